```python
import jax, jax.numpy as jnp
from jax import lax
import numpy as np

D_MODEL = 2048
BATCH = 2
SEQ = 8192
DEPTH = 1

SB_HEADS = 8
SB_HEAD_DIM = 128
SB_WIDTH = SB_HEADS * SB_HEAD_DIM
SB_BLOCK = 128
HG_HEADS = 8
HG_KEY_DIM = 128
HG_VAL_DIM = 128
HG_K_WIDTH = HG_HEADS * HG_KEY_DIM
HG_V_WIDTH = HG_HEADS * HG_VAL_DIM
HG_CHUNK = 64
IN_COLS = 3 * SB_WIDTH + 2 * HG_K_WIDTH + 2 * HG_V_WIDTH + 2 * D_MODEL
D_FF = 5632
CONV_WIDTH = 3
EPS = 1e-6

kernel_name = "hybrid_stickbreaking_hgrn2_convffn"


def rmsnorm(x, g):
    xf = x.astype(jnp.float32)
    y = xf * lax.rsqrt(jnp.mean(xf * xf, axis=-1, keepdims=True) + EPS)
    return (y * g.astype(jnp.float32)).astype(x.dtype)


def split_heads(a, n_heads):
    b, s, w = a.shape
    return a.reshape(b, s, n_heads, w // n_heads).transpose(0, 2, 1, 3)


def merge_heads(a):
    b, h, s, d = a.shape
    return a.transpose(0, 2, 1, 3).reshape(b, s, h * d)


def stick_breaking_attention(q, k, v):
    seq = q.shape[2]
    scale = SB_HEAD_DIM ** -0.5
    outs = []
    for blk in range(seq // SB_BLOCK):
        t0, t1 = blk * SB_BLOCK, (blk + 1) * SB_BLOCK
        qb = q[:, :, t0:t1].astype(jnp.float32)
        kb = k[:, :, :t1].astype(jnp.float32)
        z = jnp.einsum('bhtd,bhsd->bhts', qb, kb) * scale
        t_idx = t0 + jnp.arange(SB_BLOCK)[:, None]
        s_idx = jnp.arange(t1)[None, :]
        strict = s_idx < t_idx
        log_keep = jnp.where(strict, jax.nn.log_sigmoid(-z), 0.0)
        later = lax.cumsum(log_keep, axis=3, reverse=True) - log_keep
        log_w = jnp.where(strict, jax.nn.log_sigmoid(z) + later, -jnp.inf)
        w = jnp.exp(log_w)
        outs.append(jnp.einsum('bhts,bhsd->bhtd', w, v[:, :, :t1].astype(jnp.float32)))
    return jnp.concatenate(outs, axis=2).astype(v.dtype)


def hgrn2_chunkwise(q, k, v, log_f):
    b_, h_, seq, dk = q.shape
    dv = v.shape[-1]
    n_chunks = seq // HG_CHUNK

    def to_chunks(a):
        return jnp.moveaxis(a.astype(jnp.float32).reshape(b_, h_, n_chunks, HG_CHUNK, a.shape[-1]), 2, 0)

    qc, kc, vc, gc = to_chunks(q), to_chunks(k), to_chunks(v), to_chunks(log_f)
    causal = jnp.tril(jnp.ones((HG_CHUNK, HG_CHUNK), dtype=bool))[:, :, None]

    def step(state, inp):
        qi, ki, vi, gi = inp
        b = jnp.cumsum(gi, axis=2)
        o_inter = jnp.einsum('bhtd,bhde->bhte', qi * jnp.exp(b), state)
        diff = b[:, :, :, None, :] - b[:, :, None, :, :]
        decay = jnp.exp(jnp.where(causal, diff, -jnp.inf))
        scores = jnp.einsum('bhtd,bhsd,bhtsd->bhts', qi, ki, decay)
        o_intra = jnp.einsum('bhts,bhse->bhte', scores, vi)
        b_last = b[:, :, -1:, :]
        k_dec = ki * jnp.exp(b_last - b)
        new_state = jnp.exp(b_last[:, :, 0, :])[..., None] * state + jnp.einsum('bhsd,bhse->bhde', k_dec, vi)
        return new_state, o_inter + o_intra

    state0 = jnp.zeros((b_, h_, dk, dv), jnp.float32)
    _, o = lax.scan(step, state0, (qc, kc, vc, gc))
    return jnp.moveaxis(o, 0, 2).reshape(b_, h_, seq, dv)


def causal_depthwise_conv(a, w, bias):
    seq = a.shape[1]
    a_pad = jnp.pad(a, ((0, 0), (CONV_WIDTH - 1, 0), (0, 0)))
    out = bias
    for j in range(CONV_WIDTH):
        out = out + w[j] * a_pad[:, j:j + seq]
    return out


def setup_inputs(seed: int = 0) -> dict:
    key = jax.random.key(seed)
    ks = jax.random.split(key, 16)
    f32 = jnp.float32
    nrm = lambda k, shape, s: jax.random.normal(k, shape, f32) * s
    return {
        "x": nrm(ks[0], (BATCH, SEQ, D_MODEL), 1.0),
        "g_mix": 1.0 + nrm(ks[1], (DEPTH, D_MODEL), 0.02),
        "w_in": nrm(ks[2], (DEPTH, D_MODEL, IN_COLS), D_MODEL ** -0.5),
        "g_q": 1.0 + nrm(ks[3], (DEPTH, SB_HEAD_DIM), 0.02),
        "g_k": 1.0 + nrm(ks[4], (DEPTH, SB_HEAD_DIM), 0.02),
        "lb_logits": nrm(ks[5], (DEPTH + 1, HG_K_WIDTH), 0.5),
        "g_hg_out": 1.0 + nrm(ks[6], (DEPTH, HG_VAL_DIM), 0.02),
        "p_a": nrm(ks[7], (DEPTH, SB_WIDTH, D_MODEL), SB_WIDTH ** -0.5),
        "p_b": nrm(ks[8], (DEPTH, HG_V_WIDTH, D_MODEL), HG_V_WIDTH ** -0.5),
        "w_o": nrm(ks[9], (DEPTH, D_MODEL, D_MODEL), D_MODEL ** -0.5),
        "g_ffn": 1.0 + nrm(ks[10], (DEPTH, D_MODEL), 0.02),
        "w_up": nrm(ks[11], (DEPTH, D_MODEL, 2 * D_FF), D_MODEL ** -0.5),
        "conv_w": nrm(ks[12], (DEPTH, CONV_WIDTH, 2 * D_FF), CONV_WIDTH ** -0.5),
        "conv_b": nrm(ks[13], (DEPTH, 2 * D_FF), 0.02),
        "w_down": nrm(ks[14], (DEPTH, D_FF, D_MODEL), D_FF ** -0.5),
    }


def reference(x, g_mix, w_in, g_q, g_k, lb_logits, g_hg_out, p_a, p_b, w_o,
              g_ffn, w_up, conv_w, conv_b, w_down):
    splits = [SB_WIDTH, 2 * SB_WIDTH, 3 * SB_WIDTH,
              3 * SB_WIDTH + HG_K_WIDTH, 3 * SB_WIDTH + 2 * HG_K_WIDTH,
              3 * SB_WIDTH + 2 * HG_K_WIDTH + HG_V_WIDTH,
              3 * SB_WIDTH + 2 * HG_K_WIDTH + 2 * HG_V_WIDTH,
              3 * SB_WIDTH + 2 * HG_K_WIDTH + 2 * HG_V_WIDTH + D_MODEL]
    lower_bounds = jnp.cumsum(jax.nn.softmax(lb_logits.astype(jnp.float32), axis=0), axis=0)
    h = x
    for layer in range(DEPTH):
        u = rmsnorm(h, g_mix[layer])
        proj = u @ w_in[layer]
        sb_q, sb_k, sb_v, hg_q, hg_f, hg_i, hg_og, gate_a, gate_b = jnp.split(proj, splits, axis=-1)

        q = rmsnorm(split_heads(sb_q, SB_HEADS), g_q[layer])
        k = rmsnorm(split_heads(sb_k, SB_HEADS), g_k[layer])
        v = split_heads(sb_v, SB_HEADS)
        y_a = merge_heads(stick_breaking_attention(q, k, v))

        lb = lower_bounds[layer].reshape(1, HG_HEADS, 1, HG_KEY_DIM)
        f = lb + (1.0 - lb) * jax.nn.sigmoid(split_heads(hg_f, HG_HEADS).astype(jnp.float32))
        o_b = hgrn2_chunkwise(jax.nn.silu(split_heads(hg_q, HG_HEADS)), 1.0 - f,
                              split_heads(hg_i, HG_HEADS), jnp.log(f))
        o_b = rmsnorm(o_b, g_hg_out[layer])
        y_b = (merge_heads(o_b) * jax.nn.silu(hg_og.astype(jnp.float32))).astype(h.dtype)

        m = jax.nn.sigmoid(gate_a) * (y_a @ p_a[layer]) + jax.nn.sigmoid(gate_b) * (y_b @ p_b[layer])
        h = h + m @ w_o[layer]

        up = rmsnorm(h, g_ffn[layer]) @ w_up[layer]
        up = causal_depthwise_conv(up, conv_w[layer], conv_b[layer])
        gate, val = jnp.split(up, 2, axis=-1)
        h = h + (jax.nn.silu(gate) * val) @ w_down[layer]
    return h
```

```python
import functools

import numpy as np
import jax
import jax.numpy as jnp
from jax import lax
from jax.experimental import pallas as pl
from jax.experimental.pallas import tpu as pltpu

F32 = jnp.float32
BF16 = jnp.bfloat16

EPS = 1e-6
HEAD_DIM = 128
N_HEADS = 8
MIX_WIDTH = N_HEADS * HEAD_DIM
HG_CHUNK = 64
CONV_WIDTH = 3
CONV_HALO = 8
EXP_UNDERFLOW = -104.0
V7X_VMEM_LIMIT = 56 * 1024 * 1024


def _params(semantics, vmem_bytes=V7X_VMEM_LIMIT):
    return pltpu.CompilerParams(dimension_semantics=semantics, vmem_limit_bytes=vmem_bytes)


def _dot(a, b):
    return jnp.dot(a, b, preferred_element_type=F32)


def _dot_nt(a, b):
    return lax.dot_general(a, b, (((1,), (1,)), ((), ())), preferred_element_type=F32)


def _dot_tn(a, b):
    return lax.dot_general(a, b, (((0,), (0,)), ((), ())), preferred_element_type=F32)


def _sigmoid(x):
    return 1.0 / (1.0 + jnp.exp(-x))


def _rmsnorm_kernel(x_ref, g_ref, o_ref):
    x = x_ref[...]
    y = x * lax.rsqrt(jnp.mean(x * x, axis=-1, keepdims=True) + EPS)
    o_ref[...] = (y * g_ref[...]).astype(o_ref.dtype)


def _rmsnorm(x, g, tm):
    t, d = x.shape
    return pl.pallas_call(
        _rmsnorm_kernel,
        grid=(t // tm,),
        in_specs=[pl.BlockSpec((tm, d), lambda i: (i, 0)),
                  pl.BlockSpec((1, d), lambda i: (0, 0))],
        out_specs=pl.BlockSpec((tm, d), lambda i: (i, 0)),
        out_shape=jax.ShapeDtypeStruct((t, d), BF16),
        compiler_params=_params(("parallel",)),
        name="rmsnorm",
    )(x, g.reshape(1, d))


def _proj_kernel(u_ref, w_ref, *rest, kind):
    acc = _dot(u_ref[...], w_ref[...])
    if kind == "headnorm":
        g_ref, o_ref = rest
        for hh in range(acc.shape[1] // HEAD_DIM):
            sl = slice(hh * HEAD_DIM, (hh + 1) * HEAD_DIM)
            a = acc[:, sl]
            y = a * lax.rsqrt(jnp.mean(a * a, axis=-1, keepdims=True) + EPS)
            o_ref[:, sl] = (y * g_ref[:, sl]).astype(o_ref.dtype)
    elif kind == "sigmoid":
        (o_ref,) = rest
        o_ref[...] = _sigmoid(acc).astype(o_ref.dtype)
    else:
        (o_ref,) = rest
        o_ref[...] = acc.astype(o_ref.dtype)


def _proj(u, w, col_blocks, tn, out_dtype, kind, gain=None, tm=1024):
    t, k = u.shape
    tm = min(tm, t)
    nb = len(col_blocks)

    def wcol(n):
        idx = jnp.int32(col_blocks[0])
        for i in range(1, nb):
            idx = jnp.where(n == i, jnp.int32(col_blocks[i]), idx)
        return idx

    in_specs = [pl.BlockSpec((tm, k), lambda m, n: (m, 0)),
                pl.BlockSpec((k, tn), lambda m, n: (0, wcol(n)))]
    args = [u, w]
    if kind == "headnorm":
        in_specs.append(pl.BlockSpec((1, tn), lambda m, n: (0, n)))
        args.append(gain)
    return pl.pallas_call(
        functools.partial(_proj_kernel, kind=kind),
        grid=(t // tm, nb),
        in_specs=in_specs,
        out_specs=pl.BlockSpec((tm, tn), lambda m, n: (m, n)),
        out_shape=jax.ShapeDtypeStruct((t, nb * tn), out_dtype),
        compiler_params=_params(("parallel", "arbitrary")),
        name="proj_" + kind,
    )(*args)


def _sb_kernel(q_ref, k_ref, v_ref, o_ref, *, tq, tk, scale):
    i = pl.program_id(2)
    nd = tq // tk
    q = q_ref[...]
    jj = lax.broadcasted_iota(jnp.int32, (tk, tk), 0)
    ss = lax.broadcasted_iota(jnp.int32, (tk, tk), 1)
    upper = jnp.where(jj > ss, 1.0, 0.0).astype(BF16)
    trow = lax.broadcasted_iota(jnp.int32, (tq, tk), 0)
    scol = lax.broadcasted_iota(jnp.int32, (tq, tk), 1)

    def step(k, v, carry, acc, mask):
        z = _dot_nt(q, k) * scale
        sp = jnp.maximum(z, 0.0) + jnp.log1p(jnp.exp(-jnp.abs(z)))
        lk = -sp
        if mask is not None:
            lk = jnp.where(mask, lk, 0.0)
        lk_hi = lk.astype(BF16)
        lk_lo = (lk - lk_hi.astype(F32)).astype(BF16)
        later = _dot(lk_hi, upper) + _dot(lk_lo, upper)
        w = jnp.exp((z - sp) + later + carry)
        if mask is not None:
            w = jnp.where(mask, w, 0.0)
        acc = acc + _dot(w.astype(BF16), v)
        carry = carry + (later[:, :1] + lk[:, :1])
        return carry, acc

    carry = jnp.zeros((tq, 1), F32)
    acc = jnp.zeros((tq, HEAD_DIM), F32)
    for d in range(nd - 1, -1, -1):
        off = pl.multiple_of((i * nd + d) * tk, tk)
        mask = (scol + d * tk) < trow
        carry, acc = step(k_ref[pl.ds(off, tk), :], v_ref[pl.ds(off, tk), :], carry, acc, mask)

    def cond(c):
        j, top, _, _ = c
        return jnp.logical_and(j >= 0, top > EXP_UNDERFLOW)

    def body(c):
        j, _, carry, acc = c
        off = pl.multiple_of(j * tk, tk)
        carry, acc = step(k_ref[pl.ds(off, tk), :], v_ref[pl.ds(off, tk), :], carry, acc, None)
        return j - 1, jnp.max(carry), carry, acc

    _, _, _, acc = lax.while_loop(cond, body, (i * nd - 1, jnp.max(carry), carry, acc))
    o_ref[...] = acc.astype(o_ref.dtype)


def _sb_attention(qk, vhi, batch, seq, tq=256, tk=128):
    t = batch * seq
    tq = min(tq, seq)
    nq = seq // tq
    return pl.pallas_call(
        functools.partial(_sb_kernel, tq=tq, tk=tk, scale=HEAD_DIM ** -0.5),
        grid=(batch, N_HEADS, nq),
        in_specs=[pl.BlockSpec((tq, HEAD_DIM), lambda b, h, i: (b * nq + i, h)),
                  pl.BlockSpec((seq, HEAD_DIM), lambda b, h, i: (b, N_HEADS + h)),
                  pl.BlockSpec((seq, HEAD_DIM), lambda b, h, i: (b, h))],
        out_specs=pl.BlockSpec((tq, HEAD_DIM), lambda b, h, i: (b * nq + i, h)),
        out_shape=jax.ShapeDtypeStruct((t, MIX_WIDTH), BF16),
        compiler_params=_params(("parallel", "parallel", "arbitrary")),
        name="sb_attention",
    )(qk, qk, vhi)


def _hgrn_cumsum_matrix():
    c_ = HG_CHUNK
    t = np.arange(c_)[:, None]
    j = np.arange(c_)[None, :]
    blocks = [(j <= t)]
    c = c_ // 2
    while c >= 1:
        m = (t // (2 * c)) * (2 * c) + c - 1
        right = (t % (2 * c)) >= c
        blocks.append(np.where(right, (j > m) & (j <= t), (j > t) & (j <= m)))
        c //= 2
    blocks.append(j > t)
    return np.concatenate(blocks, axis=0).astype(np.float32)


def _hgrn_kernel(xq_ref, xf_ref, xo_ref, v_ref, lbl_ref, gout_ref, a_ref, o_ref, st_ref, *, n_chunks):
    c_ = HG_CHUNK

    @pl.when(pl.program_id(2) == 0)
    def _():
        st_ref[...] = jnp.zeros_like(st_ref)

    logits = lbl_ref[...]
    e = jnp.exp(logits - jnp.max(logits, axis=0, keepdims=True))
    lb = e[0:1, :] / jnp.sum(e, axis=0, keepdims=True)
    a_mat = a_ref[...]
    trow = lax.broadcasted_iota(jnp.int32, (c_, 1), 0)
    ti = lax.broadcasted_iota(jnp.int32, (c_, c_), 0)
    si = lax.broadcasted_iota(jnp.int32, (c_, c_), 1)

    for ch in range(n_chunks):
        rows = slice(ch * c_, (ch + 1) * c_)
        xq = xq_ref[rows, :]
        v = v_ref[rows, :]
        f = lb + (1.0 - lb) * _sigmoid(xf_ref[rows, :])
        kk = 1.0 - f
        g = jnp.log(f)
        qs = xq * _sigmoid(xq)
        g_hi = g.astype(BF16)
        r = g - g_hi.astype(F32)
        g_mid = r.astype(BF16)
        g_lo = (r - g_mid.astype(F32)).astype(BF16)
        cs = _dot(a_mat, g_hi) + _dot(a_mat, g_mid) + _dot(a_mat, g_lo)

        b = cs[0:c_]
        st = st_ref[...]
        o = _dot_nt((qs * jnp.exp(b)).astype(BF16), st.astype(BF16))
        s = jnp.where(ti == si, _dot_nt(qs.astype(BF16), kk.astype(BF16)), 0.0)
        c = c_ // 2
        lvl = 1
        while c >= 1:
            decay = jnp.exp(cs[lvl * c_:(lvl + 1) * c_])
            right = (trow & (2 * c - 1)) >= c
            ql = jnp.where(right, qs * decay, 0.0).astype(BF16)
            kl = jnp.where(right, 0.0, kk * decay).astype(BF16)
            sl = _dot_nt(ql, kl)
            if 2 * c < c_:
                sl = jnp.where((ti ^ si) < 2 * c, sl, 0.0)
            s = s + sl
            c //= 2
            lvl += 1
        o = o + _dot(s.astype(BF16), v)

        kd = (kk * jnp.exp(cs[7 * c_:8 * c_])).astype(BF16)
        st_ref[...] = st * jnp.exp(b[c_ - 1:c_, :]) + _dot_tn(v, kd)

        y = o * lax.rsqrt(jnp.mean(o * o, axis=-1, keepdims=True) + EPS) * gout_ref[...]
        xo = xo_ref[rows, :]
        o_ref[rows, :] = (y * (xo * _sigmoid(xo))).astype(o_ref.dtype)


def _hgrn2(hgf, vhi, lb_logits, g_out, batch, seq, tc=512):
    t = batch * seq
    tc = min(tc, seq)
    ns = seq // tc
    a_mat = jnp.asarray(_hgrn_cumsum_matrix(), dtype=BF16)
    row = lambda off: (lambda b, h, i: (b * ns + i, off + h))
    return pl.pallas_call(
        functools.partial(_hgrn_kernel, n_chunks=tc // HG_CHUNK),
        grid=(batch, N_HEADS, ns),
        in_specs=[pl.BlockSpec((tc, HEAD_DIM), row(0)),
                  pl.BlockSpec((tc, HEAD_DIM), row(N_HEADS)),
                  pl.BlockSpec((tc, HEAD_DIM), row(2 * N_HEADS)),
                  pl.BlockSpec((tc, HEAD_DIM), row(N_HEADS)),
                  pl.BlockSpec((lb_logits.shape[0], HEAD_DIM), lambda b, h, i: (0, h)),
                  pl.BlockSpec((1, HEAD_DIM), lambda b, h, i: (0, 0)),
                  pl.BlockSpec(a_mat.shape, lambda b, h, i: (0, 0))],
        out_specs=pl.BlockSpec((tc, HEAD_DIM), row(0)),
        out_shape=jax.ShapeDtypeStruct((t, MIX_WIDTH), BF16),
        scratch_shapes=[pltpu.VMEM((HEAD_DIM, HEAD_DIM), F32)],
        compiler_params=_params(("parallel", "parallel", "arbitrary")),
        name="hgrn2",
    )(hgf, hgf, hgf, vhi, lb_logits, g_out.reshape(1, HEAD_DIM), a_mat)


def _merge_kernel(ya_ref, yb_ref, pa_ref, pb_ref, ga_ref, gb_ref, o_ref):
    ma = _dot(ya_ref[...], pa_ref[...])
    mb = _dot(yb_ref[...], pb_ref[...])
    o_ref[...] = (ga_ref[...].astype(F32) * ma + gb_ref[...].astype(F32) * mb).astype(o_ref.dtype)


def _merge(ya, yb, pa, pb, gates, tm=1024, tn=1024):
    t, k = ya.shape
    d = pa.shape[1]
    tm = min(tm, t)
    nn = d // tn
    return pl.pallas_call(
        _merge_kernel,
        grid=(t // tm, nn),
        in_specs=[pl.BlockSpec((tm, k), lambda m, n: (m, 0)),
                  pl.BlockSpec((tm, k), lambda m, n: (m, 0)),
                  pl.BlockSpec((k, tn), lambda m, n: (0, n)),
                  pl.BlockSpec((k, tn), lambda m, n: (0, n)),
                  pl.BlockSpec((tm, tn), lambda m, n: (m, n)),
                  pl.BlockSpec((tm, tn), lambda m, n: (m, nn + n))],
        out_specs=pl.BlockSpec((tm, tn), lambda m, n: (m, n)),
        out_shape=jax.ShapeDtypeStruct((t, d), BF16),
        compiler_params=_params(("parallel", "arbitrary")),
        name="merge",
    )(ya, yb, pa, pb, gates, gates)


def _out_proj_kernel(m_ref, w_ref, x_ref, g_ref, h_ref, hn_ref):
    h = x_ref[...] + _dot(m_ref[...], w_ref[...])
    h_ref[...] = h
    y = h * lax.rsqrt(jnp.mean(h * h, axis=-1, keepdims=True) + EPS)
    hn_ref[...] = (y * g_ref[...]).astype(hn_ref.dtype)


def _out_proj(m, w, x, g, tm=512):
    t, d = x.shape
    tm = min(tm, t)
    return pl.pallas_call(
        _out_proj_kernel,
        grid=(t // tm,),
        in_specs=[pl.BlockSpec((tm, d), lambda i: (i, 0)),
                  pl.BlockSpec((d, d), lambda i: (0, 0)),
                  pl.BlockSpec((tm, d), lambda i: (i, 0)),
                  pl.BlockSpec((1, d), lambda i: (0, 0))],
        out_specs=[pl.BlockSpec((tm, d), lambda i: (i, 0)),
                   pl.BlockSpec((tm, d), lambda i: (i, 0))],
        out_shape=[jax.ShapeDtypeStruct((t, d), F32), jax.ShapeDtypeStruct((t, d), BF16)],
        compiler_params=_params(("parallel",)),
        name="out_proj",
    )(m, w, x, g.reshape(1, d))


def _ffn_up_kernel(hn_ref, wg_ref, wv_ref, cwg_ref, cwv_ref, cbg_ref, cbv_ref, o_ref, eg_ref, ev_ref,
                   *, tm, tiles_per_seq):
    m = pl.program_id(1)
    hn = hn_ref[...]

    @pl.when(m % tiles_per_seq == 0)
    def _():
        eg_ref[0:CONV_HALO, :] = jnp.zeros((CONV_HALO, eg_ref.shape[1]), F32)
        ev_ref[0:CONV_HALO, :] = jnp.zeros((CONV_HALO, ev_ref.shape[1]), F32)

    def conv(w_ref, cw_ref, cb_ref, e_ref):
        up = _dot(hn, w_ref[...])
        e_ref[CONV_HALO:CONV_HALO + tm, :] = up
        out = cb_ref[...] + cw_ref[0:1, :] * e_ref[CONV_HALO - 2:CONV_HALO - 2 + tm, :]
        out = out + cw_ref[1:2, :] * e_ref[CONV_HALO - 1:CONV_HALO - 1 + tm, :]
        out = out + cw_ref[2:3, :] * up
        e_ref[0:CONV_HALO, :] = e_ref[tm:tm + CONV_HALO, :]
        return out

    gate = conv(wg_ref, cwg_ref, cbg_ref, eg_ref)
    val = conv(wv_ref, cwv_ref, cbv_ref, ev_ref)
    o_ref[...] = (gate * _sigmoid(gate) * val).astype(o_ref.dtype)


def _ffn_up(hn, w_up, conv_w, conv_b, seq, tm=1024, tn=512):
    t, d = hn.shape
    dff = w_up.shape[1] // 2
    tm = min(tm, seq)
    nn = dff // tn
    return pl.pallas_call(
        functools.partial(_ffn_up_kernel, tm=tm, tiles_per_seq=seq // tm),
        grid=(nn, t // tm),
        in_specs=[pl.BlockSpec((tm, d), lambda n, m: (m, 0)),
                  pl.BlockSpec((d, tn), lambda n, m: (0, n)),
                  pl.BlockSpec((d, tn), lambda n, m: (0, nn + n)),
                  pl.BlockSpec((CONV_WIDTH, tn), lambda n, m: (0, n)),
                  pl.BlockSpec((CONV_WIDTH, tn), lambda n, m: (0, nn + n)),
                  pl.BlockSpec((1, tn), lambda n, m: (0, n)),
                  pl.BlockSpec((1, tn), lambda n, m: (0, nn + n))],
        out_specs=pl.BlockSpec((tm, tn), lambda n, m: (m, n)),
        out_shape=jax.ShapeDtypeStruct((t, dff), BF16),
        scratch_shapes=[pltpu.VMEM((tm + CONV_HALO, tn), F32), pltpu.VMEM((tm + CONV_HALO, tn), F32)],
        compiler_params=_params(("parallel", "arbitrary")),
        name="ffn_up",
    )(hn, w_up, w_up, conv_w, conv_w, conv_b.reshape(1, -1), conv_b.reshape(1, -1))


def _ffn_down_kernel(a_ref, w_ref, h_ref, o_ref):
    o_ref[...] = h_ref[...] + _dot(a_ref[...], w_ref[...])


def _ffn_down(act, w, h, tm=512, tn=1024):
    t, k = act.shape
    d = w.shape[1]
    tm = min(tm, t)
    return pl.pallas_call(
        _ffn_down_kernel,
        grid=(d // tn, t // tm),
        in_specs=[pl.BlockSpec((tm, k), lambda n, m: (m, 0)),
                  pl.BlockSpec((k, tn), lambda n, m: (0, n)),
                  pl.BlockSpec((tm, tn), lambda n, m: (m, n))],
        out_specs=pl.BlockSpec((tm, tn), lambda n, m: (m, n)),
        out_shape=jax.ShapeDtypeStruct((t, d), F32),
        compiler_params=_params(("parallel", "arbitrary")),
        name="ffn_down",
    )(act, w, h)


def kernel(x, g_mix, w_in, g_q, g_k, lb_logits, g_hg_out, p_a, p_b, w_o, g_ffn, w_up, conv_w, conv_b, w_down):
    batch, seq, d = x.shape
    depth = g_mix.shape[0]
    assert depth == 1 and lb_logits.shape == (depth + 1, MIX_WIDTH)
    h = x.reshape(batch * seq, d)
    for layer in range(depth):
        w_in_l = w_in[layer].astype(BF16)
        u = _rmsnorm(h, g_mix[layer], tm=min(512, seq))
        qk_gain = jnp.concatenate([jnp.tile(g_q[layer], N_HEADS), jnp.tile(g_k[layer], N_HEADS)]).reshape(1, -1)
        qk = _proj(u, w_in_l, (0, 1), MIX_WIDTH, BF16, "headnorm", gain=qk_gain)
        vhi = _proj(u, w_in_l, (2, 5), MIX_WIDTH, BF16, "cast")
        hgf = _proj(u, w_in_l, (3, 4, 6), MIX_WIDTH, F32, "cast")
        gates = _proj(u, w_in_l, (7, 8, 9, 10), MIX_WIDTH, BF16, "sigmoid")

        y_a = _sb_attention(qk, vhi, batch, seq)
        y_b = _hgrn2(hgf, vhi, lb_logits, g_hg_out[layer], batch, seq)

        m = _merge(y_a, y_b, p_a[layer].astype(BF16), p_b[layer].astype(BF16), gates)
        h, hn = _out_proj(m, w_o[layer].astype(BF16), h, g_ffn[layer])
        act = _ffn_up(hn, w_up[layer].astype(BF16), conv_w[layer], conv_b[layer], seq)
        h = _ffn_down(act, w_down[layer].astype(BF16), h)
    return h.reshape(batch, seq, d)
```

```python
import functools

import numpy as np
import jax
import jax.numpy as jnp
from jax import lax
from jax.experimental import pallas as pl
from jax.experimental.pallas import tpu as pltpu

F32 = jnp.float32
BF16 = jnp.bfloat16

EPS = 1e-6
HEAD_DIM = 128
N_HEADS = 8
MIX_WIDTH = N_HEADS * HEAD_DIM
HG_CHUNK = 64
CONV_WIDTH = 3
CONV_HALO = 8
EXP2_UNDERFLOW = 150.0
LOG2E = 1.4426950408889634
SIGN_BIT = np.int32(-2 ** 31)
BF16_BITS = np.int32(-2 ** 16)
V7X_VMEM_LIMIT = 56 * 1024 * 1024


def _params(semantics, vmem_bytes=V7X_VMEM_LIMIT):
    return pltpu.CompilerParams(dimension_semantics=semantics, vmem_limit_bytes=vmem_bytes)


def _dot(a, b):
    return jnp.dot(a, b, preferred_element_type=F32)


def _dot_nt(a, b):
    return lax.dot_general(a, b, (((1,), (1,)), ((), ())), preferred_element_type=F32)


def _dot_tn(a, b):
    return lax.dot_general(a, b, (((0,), (0,)), ((), ())), preferred_element_type=F32)


def _sigmoid(x):
    return 1.0 / (1.0 + jnp.exp(-x))


def _bits(x):
    return lax.bitcast_convert_type(x, jnp.int32)


def _from_bits(x):
    return lax.bitcast_convert_type(x, F32)


def _rmsnorm_kernel(x_ref, g_ref, o_ref):
    x = x_ref[...]
    y = x * lax.rsqrt(jnp.mean(x * x, axis=-1, keepdims=True) + EPS)
    o_ref[...] = (y * g_ref[...]).astype(o_ref.dtype)


def _rmsnorm(x, g, tm):
    t, d = x.shape
    return pl.pallas_call(
        _rmsnorm_kernel,
        grid=(t // tm,),
        in_specs=[pl.BlockSpec((tm, d), lambda i: (i, 0)),
                  pl.BlockSpec((1, d), lambda i: (0, 0))],
        out_specs=pl.BlockSpec((tm, d), lambda i: (i, 0)),
        out_shape=jax.ShapeDtypeStruct((t, d), BF16),
        compiler_params=_params(("parallel",)),
        name="rmsnorm",
    )(x, g.reshape(1, d))


def _proj_kernel(u_ref, w_ref, *rest, kind):
    acc = _dot(u_ref[...], w_ref[...])
    if kind == "headnorm":
        g_ref, o_ref = rest
        for hh in range(acc.shape[1] // HEAD_DIM):
            sl = slice(hh * HEAD_DIM, (hh + 1) * HEAD_DIM)
            a = acc[:, sl]
            y = a * lax.rsqrt(jnp.mean(a * a, axis=-1, keepdims=True) + EPS)
            o_ref[:, sl] = (y * g_ref[:, sl]).astype(o_ref.dtype)
    elif kind == "sigmoid":
        (o_ref,) = rest
        o_ref[...] = _sigmoid(acc).astype(o_ref.dtype)
    else:
        (o_ref,) = rest
        o_ref[...] = acc.astype(o_ref.dtype)


def _proj(u, w, col_blocks, tn, out_dtype, kind, gain=None, tm=1024):
    t, k = u.shape
    tm = min(tm, t)
    nb = len(col_blocks)

    def wcol(n):
        idx = jnp.int32(col_blocks[0])
        for i in range(1, nb):
            idx = jnp.where(n == i, jnp.int32(col_blocks[i]), idx)
        return idx

    in_specs = [pl.BlockSpec((tm, k), lambda m, n: (m, 0)),
                pl.BlockSpec((k, tn), lambda m, n: (0, wcol(n)))]
    args = [u, w]
    if kind == "headnorm":
        in_specs.append(pl.BlockSpec((1, tn), lambda m, n: (0, n)))
        args.append(gain)
    return pl.pallas_call(
        functools.partial(_proj_kernel, kind=kind),
        grid=(t // tm, nb),
        in_specs=in_specs,
        out_specs=pl.BlockSpec((tm, tn), lambda m, n: (m, n)),
        out_shape=jax.ShapeDtypeStruct((t, nb * tn), out_dtype),
        compiler_params=_params(("parallel", "arbitrary")),
        name="proj_" + kind,
    )(*args)


def _sb_kernel(q_ref, k_ref, v_ref, u_ref, o_ref, *, tq, tk, n_heads, n_ahead, scale):
    i = pl.program_id(2)
    nd = tq // tk
    heads = range(n_heads)
    hcol = lambda hd: slice(hd * HEAD_DIM, (hd + 1) * HEAD_DIM)
    qs = [q_ref[:, hcol(hd)] for hd in heads]
    upper2 = u_ref[...]
    trow = lax.broadcasted_iota(jnp.int32, (tq, tk), 0)
    scol = lax.broadcasted_iota(jnp.int32, (tq, tk), 1)

    def step(blocks, carries, accs):
        pairs = [(bi, hd) for bi in range(len(blocks)) for hd in heads]
        z_, sp_, lb_, part_, later_ = {}, {}, {}, {}, {}
        for bi, hd in pairs:
            z_[bi, hd] = _dot_nt(qs[hd], k_ref[pl.ds(blocks[bi][0], tk), hcol(hd)]) * (scale * LOG2E)
        for bi, hd in pairs:
            z, mask = z_[bi, hd], blocks[bi][1]
            neg_abs = _from_bits(_bits(z) | SIGN_BIT)
            sp = jnp.maximum(z, 0.0) + jnp.log2(1.0 + jnp.exp2(neg_abs))
            lb_[bi, hd] = z - sp
            if mask is not None:
                sp = jnp.where(mask, sp, 0.0)
            sp_hi = _from_bits(_bits(sp) & BF16_BITS)
            sp_[bi, hd] = sp
            part_[bi, hd] = jnp.concatenate([sp_hi.astype(BF16), (sp - sp_hi).astype(BF16)], axis=1)
        for p in pairs:
            later_[p] = _dot(part_[p], upper2)
        carries, accs = list(carries), list(accs)
        for bi, hd in pairs:
            mask = blocks[bi][1]
            w = jnp.exp2(lb_[bi, hd] - later_[bi, hd] - carries[hd])
            w = (w if mask is None else jnp.where(mask, w, 0.0)).astype(BF16)
            accs[hd] = accs[hd] + _dot(w, v_ref[pl.ds(blocks[bi][0], tk), hcol(hd)])
            carries[hd] = carries[hd] + (later_[bi, hd][:, :1] + sp_[bi, hd][:, :1])
        return tuple(carries), tuple(accs)

    def least_of(carries):
        m = carries[0]
        for c in carries[1:]:
            m = jnp.minimum(m, c)
        return jnp.min(m)

    zero = (tuple(jnp.zeros((tq, 1), F32) for _ in heads), tuple(jnp.zeros((tq, HEAD_DIM), F32) for _ in heads))
    first = i * nd
    diag = [(pl.multiple_of((first + d) * tk, tk), (scol + d * tk) < trow) for d in range(nd - 1, -1, -1)]
    ahead = [(pl.multiple_of((first - 1 - a) * tk, tk), None) for a in range(n_ahead)]
    carries, accs = lax.cond(first >= n_ahead, lambda: step(diag + ahead, *zero), lambda: step(diag, *zero))
    start = jnp.where(first >= n_ahead, first - 1 - n_ahead, first - 1)

    def cond(c):
        return jnp.logical_and(c[0] >= 0, c[1] < EXP2_UNDERFLOW)

    def body(c):
        j, _, carries, accs = c
        carries, accs = step([(pl.multiple_of(j * tk, tk), None)], carries, accs)
        return j - 1, least_of(carries), carries, accs

    _, _, _, accs = lax.while_loop(cond, body, (start, least_of(carries), carries, accs))
    for hd in heads:
        o_ref[:, hcol(hd)] = accs[hd].astype(o_ref.dtype)


def _sb_attention(qk, vhi, batch, seq, tq=128, tk=128, hp=4, n_ahead=2):
    t = batch * seq
    tq = min(tq, seq)
    nq = seq // tq
    ng = N_HEADS // hp
    w = hp * HEAD_DIM
    tri = np.tril(np.ones((tk, tk), np.float32), -1)
    upper2 = jnp.asarray(np.concatenate([tri, tri], axis=0), dtype=BF16)
    return pl.pallas_call(
        functools.partial(_sb_kernel, tq=tq, tk=tk, n_heads=hp, n_ahead=n_ahead, scale=HEAD_DIM ** -0.5),
        grid=(batch, ng, nq),
        in_specs=[pl.BlockSpec((tq, w), lambda b, h, i: (b * nq + i, h)),
                  pl.BlockSpec((seq, w), lambda b, h, i: (b, ng + h)),
                  pl.BlockSpec((seq, w), lambda b, h, i: (b, h)),
                  pl.BlockSpec((2 * tk, tk), lambda b, h, i: (0, 0))],
        out_specs=pl.BlockSpec((tq, w), lambda b, h, i: (b * nq + i, h)),
        out_shape=jax.ShapeDtypeStruct((t, MIX_WIDTH), BF16),
        compiler_params=_params(("parallel", "parallel", "arbitrary")),
        name="sb_attention",
    )(qk, qk, vhi, upper2)


def _hgrn_cumsum_matrix():
    tri = np.tril(np.ones((HG_CHUNK, HG_CHUNK), np.float32))
    return np.concatenate([tri, tri, tri], axis=1)


def _hgrn_kernel(xq_ref, xf_ref, xo_ref, v_ref, lbl_ref, gout_ref, a_ref, o_ref, st_ref, *, n_chunks, n_heads):
    c_ = HG_CHUNK

    @pl.when(pl.program_id(2) == 0)
    def _():
        st_ref[...] = jnp.zeros_like(st_ref)

    logits = lbl_ref[...]
    e = jnp.exp(logits - jnp.max(logits, axis=0, keepdims=True))
    lb_all = e[0:1, :] / jnp.sum(e, axis=0, keepdims=True)
    a_mat = a_ref[...]
    ti = lax.broadcasted_iota(jnp.int32, (c_, c_), 0)
    si = lax.broadcasted_iota(jnp.int32, (c_, c_), 1)
    tr = lax.broadcasted_iota(jnp.int32, (c_, HEAD_DIM), 0)
    pair_of = {c: ((ti ^ si) < 2 * c) & ((ti & (2 * c - 1)) >= c) & ((si & (2 * c - 1)) < c)
               for c in (32, 16, 8, 4, 2, 1)}
    diag = ti == si
    r4 = tr & 3
    odd = (tr & 1) == 1

    def level_decays(b2, f):
        out = {}
        for c in (32, 16, 8, 4):
            bm = jnp.concatenate(
                [jnp.broadcast_to(b2[s0 + c - 1:s0 + c, :], (2 * c, HEAD_DIM)) for s0 in range(0, c_, 2 * c)], axis=0)
            out[c] = jnp.exp2(_from_bits(_bits(b2 - bm) | SIGN_BIT))
        f_prev = pltpu.roll(f, 1, axis=0)
        f_next = pltpu.roll(f, c_ - 1, axis=0)
        out[2] = jnp.where(r4 == 0, f_next, jnp.where(r4 == 1, 1.0, jnp.where(r4 == 2, f, f_prev * f)))
        out[1] = jnp.where(odd, f, 1.0)
        return out

    pairs = [(ch, hd) for ch in range(n_chunks) for hd in range(n_heads)]
    blk = lambda ch, hd: (slice(ch * c_, (ch + 1) * c_), slice(hd * HEAD_DIM, (hd + 1) * HEAD_DIM))
    f_, kk_, qs_, g3_, b_, s_ = {}, {}, {}, {}, {}, {}
    o_intra, q_in, d_state, e_last = {}, {}, {}, {}
    for p in pairs:
        rows, cols = blk(*p)
        lb = lb_all[:, cols]
        xq = xq_ref[rows, cols]
        f = lb + (1.0 - lb) * _sigmoid(xf_ref[rows, cols])
        g = jnp.log(f)
        g_hi = g.astype(BF16)
        r = g - g_hi.astype(F32)
        g_mid = r.astype(BF16)
        g_lo = (r - g_mid.astype(F32)).astype(BF16)
        f_[p], kk_[p], qs_[p] = f, 1.0 - f, xq * _sigmoid(xq)
        g3_[p] = jnp.concatenate([g_hi, g_mid, g_lo], axis=0)
    for p in pairs:
        b_[p] = _dot(a_mat, g3_[p]) * LOG2E
    for p in pairs:
        qs, kk = qs_[p], kk_[p]
        s = jnp.where(diag, _dot_nt(qs.astype(BF16), kk.astype(BF16)), 0.0)
        for c, decay in level_decays(b_[p], f_[p]).items():
            sl = _dot_nt((qs * decay).astype(BF16), (kk * decay).astype(BF16))
            s = s + jnp.where(pair_of[c], sl, 0.0)
        s_[p] = s.astype(BF16)
    for p in pairs:
        rows, cols = blk(*p)
        v = v_ref[rows, cols]
        b2 = b_[p]
        b2_last = b2[c_ - 1:c_, :]
        o_intra[p] = _dot(s_[p], v)
        q_in[p] = (qs_[p] * jnp.exp2(b2)).astype(BF16)
        d_state[p] = _dot_tn(v, (kk_[p] * jnp.exp2(b2_last - b2)).astype(BF16))
        e_last[p] = jnp.exp2(b2_last)

    for hd in range(n_heads):
        cols = slice(hd * HEAD_DIM, (hd + 1) * HEAD_DIM)
        st = st_ref[hd]
        for ch in range(n_chunks):
            rows = slice(ch * c_, (ch + 1) * c_)
            o = o_intra[ch, hd] + _dot_nt(q_in[ch, hd], st.astype(BF16))
            st = st * e_last[ch, hd] + d_state[ch, hd]
            y = o * lax.rsqrt(jnp.mean(o * o, axis=-1, keepdims=True) + EPS) * gout_ref[...]
            xo = xo_ref[rows, cols]
            o_ref[rows, cols] = (y * (xo * _sigmoid(xo))).astype(o_ref.dtype)
        st_ref[hd] = st


def _hgrn2(hgf, vhi, lb_logits, g_out, batch, seq, tc=512, hp=4):
    t = batch * seq
    tc = min(tc, seq)
    ns = seq // tc
    ng = N_HEADS // hp
    w = hp * HEAD_DIM
    a_mat = jnp.asarray(_hgrn_cumsum_matrix(), dtype=BF16)
    row = lambda off: (lambda b, h, i: (b * ns + i, off + h))
    return pl.pallas_call(
        functools.partial(_hgrn_kernel, n_chunks=tc // HG_CHUNK, n_heads=hp),
        grid=(batch, ng, ns),
        in_specs=[pl.BlockSpec((tc, w), row(0)),
                  pl.BlockSpec((tc, w), row(ng)),
                  pl.BlockSpec((tc, w), row(2 * ng)),
                  pl.BlockSpec((tc, w), row(ng)),
                  pl.BlockSpec((lb_logits.shape[0], w), lambda b, h, i: (0, h)),
                  pl.BlockSpec((1, HEAD_DIM), lambda b, h, i: (0, 0)),
                  pl.BlockSpec(a_mat.shape, lambda b, h, i: (0, 0))],
        out_specs=pl.BlockSpec((tc, w), row(0)),
        out_shape=jax.ShapeDtypeStruct((t, MIX_WIDTH), BF16),
        scratch_shapes=[pltpu.VMEM((hp, HEAD_DIM, HEAD_DIM), F32)],
        compiler_params=_params(("parallel", "parallel", "arbitrary")),
        name="hgrn2",
    )(hgf, hgf, hgf, vhi, lb_logits, g_out.reshape(1, HEAD_DIM), a_mat)


def _merge_kernel(ya_ref, yb_ref, pa_ref, pb_ref, ga_ref, gb_ref, o_ref):
    ma = _dot(ya_ref[...], pa_ref[...])
    mb = _dot(yb_ref[...], pb_ref[...])
    o_ref[...] = (ga_ref[...].astype(F32) * ma + gb_ref[...].astype(F32) * mb).astype(o_ref.dtype)


def _merge(ya, yb, pa, pb, gates, tm=1024, tn=1024):
    t, k = ya.shape
    d = pa.shape[1]
    tm = min(tm, t)
    nn = d // tn
    return pl.pallas_call(
        _merge_kernel,
        grid=(t // tm, nn),
        in_specs=[pl.BlockSpec((tm, k), lambda m, n: (m, 0)),
                  pl.BlockSpec((tm, k), lambda m, n: (m, 0)),
                  pl.BlockSpec((k, tn), lambda m, n: (0, n)),
                  pl.BlockSpec((k, tn), lambda m, n: (0, n)),
                  pl.BlockSpec((tm, tn), lambda m, n: (m, n)),
                  pl.BlockSpec((tm, tn), lambda m, n: (m, nn + n))],
        out_specs=pl.BlockSpec((tm, tn), lambda m, n: (m, n)),
        out_shape=jax.ShapeDtypeStruct((t, d), BF16),
        compiler_params=_params(("parallel", "arbitrary")),
        name="merge",
    )(ya, yb, pa, pb, gates, gates)


def _out_proj_kernel(m_ref, w_ref, x_ref, g_ref, h_ref, hn_ref):
    h = x_ref[...] + _dot(m_ref[...], w_ref[...])
    h_ref[...] = h
    y = h * lax.rsqrt(jnp.mean(h * h, axis=-1, keepdims=True) + EPS)
    hn_ref[...] = (y * g_ref[...]).astype(hn_ref.dtype)


def _out_proj(m, w, x, g, tm=512):
    t, d = x.shape
    tm = min(tm, t)
    return pl.pallas_call(
        _out_proj_kernel,
        grid=(t // tm,),
        in_specs=[pl.BlockSpec((tm, d), lambda i: (i, 0)),
                  pl.BlockSpec((d, d), lambda i: (0, 0)),
                  pl.BlockSpec((tm, d), lambda i: (i, 0)),
                  pl.BlockSpec((1, d), lambda i: (0, 0))],
        out_specs=[pl.BlockSpec((tm, d), lambda i: (i, 0)),
                   pl.BlockSpec((tm, d), lambda i: (i, 0))],
        out_shape=[jax.ShapeDtypeStruct((t, d), F32), jax.ShapeDtypeStruct((t, d), BF16)],
        compiler_params=_params(("parallel",)),
        name="out_proj",
    )(m, w, x, g.reshape(1, d))


def _ffn_up_kernel(hn_ref, wg_ref, wv_ref, cwg_ref, cwv_ref, cbg_ref, cbv_ref, o_ref, eg_ref, ev_ref,
                   *, tm, tr, tiles_per_seq):
    m = pl.program_id(1)

    @pl.when(m % tiles_per_seq == 0)
    def _():
        eg_ref[0:CONV_HALO, :] = jnp.zeros((CONV_HALO, eg_ref.shape[1]), F32)
        ev_ref[0:CONV_HALO, :] = jnp.zeros((CONV_HALO, ev_ref.shape[1]), F32)

    def conv(up, lo, cw_ref, cb_ref, e_ref):
        e_ref[CONV_HALO + lo:CONV_HALO + lo + tr, :] = up
        out = cb_ref[...] + cw_ref[0:1, :] * e_ref[CONV_HALO - 2 + lo:CONV_HALO - 2 + lo + tr, :]
        out = out + cw_ref[1:2, :] * e_ref[CONV_HALO - 1 + lo:CONV_HALO - 1 + lo + tr, :]
        return out + cw_ref[2:3, :] * up

    for r in range(tm // tr):
        lo = r * tr
        hn = hn_ref[lo:lo + tr, :]
        gate = conv(_dot(hn, wg_ref[...]), lo, cwg_ref, cbg_ref, eg_ref)
        val = conv(_dot(hn, wv_ref[...]), lo, cwv_ref, cbv_ref, ev_ref)
        o_ref[lo:lo + tr, :] = (gate * _sigmoid(gate) * val).astype(o_ref.dtype)
    eg_ref[0:CONV_HALO, :] = eg_ref[tm:tm + CONV_HALO, :]
    ev_ref[0:CONV_HALO, :] = ev_ref[tm:tm + CONV_HALO, :]


def _ffn_up(hn, w_up, conv_w, conv_b, seq, tm=1024, tn=512, tr=1024):
    t, d = hn.shape
    dff = w_up.shape[1] // 2
    tm = min(tm, seq)
    nn = dff // tn
    return pl.pallas_call(
        functools.partial(_ffn_up_kernel, tm=tm, tr=min(tr, tm), tiles_per_seq=seq // tm),
        grid=(nn, t // tm),
        in_specs=[pl.BlockSpec((tm, d), lambda n, m: (m, 0)),
                  pl.BlockSpec((d, tn), lambda n, m: (0, n)),
                  pl.BlockSpec((d, tn), lambda n, m: (0, nn + n)),
                  pl.BlockSpec((CONV_WIDTH, tn), lambda n, m: (0, n)),
                  pl.BlockSpec((CONV_WIDTH, tn), lambda n, m: (0, nn + n)),
                  pl.BlockSpec((1, tn), lambda n, m: (0, n)),
                  pl.BlockSpec((1, tn), lambda n, m: (0, nn + n))],
        out_specs=pl.BlockSpec((tm, tn), lambda n, m: (m, n)),
        out_shape=jax.ShapeDtypeStruct((t, dff), BF16),
        scratch_shapes=[pltpu.VMEM((tm + CONV_HALO, tn), F32), pltpu.VMEM((tm + CONV_HALO, tn), F32)],
        compiler_params=_params(("parallel", "arbitrary")),
        name="ffn_up",
    )(hn, w_up, w_up, conv_w, conv_w, conv_b.reshape(1, -1), conv_b.reshape(1, -1))


def _ffn_down_kernel(a_ref, w_ref, h_ref, o_ref):
    o_ref[...] = h_ref[...] + _dot(a_ref[...], w_ref[...])


def _ffn_down(act, w, h, tm=512, tn=1024):
    t, k = act.shape
    d = w.shape[1]
    tm = min(tm, t)
    return pl.pallas_call(
        _ffn_down_kernel,
        grid=(d // tn, t // tm),
        in_specs=[pl.BlockSpec((tm, k), lambda n, m: (m, 0)),
                  pl.BlockSpec((k, tn), lambda n, m: (0, n)),
                  pl.BlockSpec((tm, tn), lambda n, m: (m, n))],
        out_specs=pl.BlockSpec((tm, tn), lambda n, m: (m, n)),
        out_shape=jax.ShapeDtypeStruct((t, d), F32),
        compiler_params=_params(("parallel", "arbitrary")),
        name="ffn_down",
    )(act, w, h)


def kernel(x, g_mix, w_in, g_q, g_k, lb_logits, g_hg_out, p_a, p_b, w_o, g_ffn, w_up, conv_w, conv_b, w_down):
    batch, seq, d = x.shape
    depth = g_mix.shape[0]
    assert depth == 1 and lb_logits.shape == (depth + 1, MIX_WIDTH)
    h = x.reshape(batch * seq, d)
    for layer in range(depth):
        w_in_l = w_in[layer].astype(BF16)
        u = _rmsnorm(h, g_mix[layer], tm=min(512, seq))
        qk_gain = jnp.concatenate([jnp.tile(g_q[layer], N_HEADS), jnp.tile(g_k[layer], N_HEADS)]).reshape(1, -1)
        qk = _proj(u, w_in_l, (0, 1), MIX_WIDTH, BF16, "headnorm", gain=qk_gain)
        vhi = _proj(u, w_in_l, (2, 5), MIX_WIDTH, BF16, "cast")
        hgf = _proj(u, w_in_l, (3, 4, 6), MIX_WIDTH, F32, "cast")
        gates = _proj(u, w_in_l, (7, 8, 9, 10), MIX_WIDTH, BF16, "sigmoid")

        y_a = _sb_attention(qk, vhi, batch, seq)
        y_b = _hgrn2(hgf, vhi, lb_logits, g_hg_out[layer], batch, seq)

        m = _merge(y_a, y_b, p_a[layer].astype(BF16), p_b[layer].astype(BF16), gates)
        h, hn = _out_proj(m, w_o[layer].astype(BF16), h, g_ffn[layer])
        act = _ffn_up(hn, w_up[layer].astype(BF16), conv_w[layer], conv_b[layer], seq)
        h = _ffn_down(act, w_down[layer].astype(BF16), h)
    return h.reshape(batch, seq, d)
```

```python
import functools

import numpy as np
import jax
import jax.numpy as jnp
from jax import lax
from jax.experimental import pallas as pl
from jax.experimental.pallas import tpu as pltpu

F32 = jnp.float32
BF16 = jnp.bfloat16

EPS = 1e-6
HEAD_DIM = 128
N_HEADS = 8
MIX_WIDTH = N_HEADS * HEAD_DIM
HG_CHUNK = 64
CONV_WIDTH = 3
CONV_HALO = 8
EXP2_UNDERFLOW = 150.0
LOG2E = 1.4426950408889634
V7X_VMEM_LIMIT = 56 * 1024 * 1024


def _params(semantics, vmem_bytes=V7X_VMEM_LIMIT, flags=None):
    return pltpu.CompilerParams(dimension_semantics=semantics, vmem_limit_bytes=vmem_bytes, flags=flags)


def _dot(a, b):
    return jnp.dot(a, b, preferred_element_type=F32)


def _dot_nt(a, b):
    return lax.dot_general(a, b, (((1,), (1,)), ((), ())), preferred_element_type=F32)


def _dot_tn(a, b):
    return lax.dot_general(a, b, (((0,), (0,)), ((), ())), preferred_element_type=F32)


def _sigmoid(x):
    return 1.0 / (1.0 + jnp.exp(-x))


def _rms_scale(x):
    return x * lax.rsqrt(jnp.mean(x * x, axis=-1, keepdims=True) + EPS)


def _rmsnorm_kernel(x_ref, g_ref, o_ref):
    o_ref[...] = (_rms_scale(x_ref[...]) * g_ref[...]).astype(o_ref.dtype)


def _rmsnorm(x, g, tm):
    t, d = x.shape
    return pl.pallas_call(
        _rmsnorm_kernel,
        grid=(t // tm,),
        in_specs=[pl.BlockSpec((tm, d), lambda i: (i, 0)),
                  pl.BlockSpec((1, d), lambda i: (0, 0))],
        out_specs=pl.BlockSpec((tm, d), lambda i: (i, 0)),
        out_shape=jax.ShapeDtypeStruct((t, d), BF16),
        compiler_params=_params(("parallel",)),
        name="rmsnorm",
    )(x, g.reshape(1, d))


def _tiled_matmul_kernel(*refs, n_a, n_w, n_x, n_o, pairs, cast_w, epilogue):
    a_refs = refs[:n_a]
    w_refs = refs[n_a:n_a + n_w]
    x_refs = refs[n_a + n_w:n_a + n_w + n_x]
    o_refs = refs[n_a + n_w + n_x:n_a + n_w + n_x + n_o]
    scratch = refs[n_a + n_w + n_x + n_o:]
    if cast_w:
        wbf_refs, scratch = scratch[:n_w], scratch[n_w:]

        @pl.when(pl.program_id(1) == 0)
        def _():
            for w_ref, wbf in zip(w_refs, wbf_refs):
                wbf[...] = w_ref[...].astype(BF16)
        w_refs = wbf_refs

    def dot_of(ai, wi):
        return lambda rows=slice(None): _dot(a_refs[ai][rows, :], w_refs[wi][...])

    epilogue([dot_of(ai, wi) for ai, wi in pairs], x_refs, o_refs, scratch)


def _tiled_matmul(name, epilogue, a_list, w_list, x_list, o_list, *, tm, tn, n_m, n_n, pairs, cast_w=False,
                  scratch=()):
    in_specs, args = [], []
    for a in a_list:
        in_specs.append(pl.BlockSpec((tm, a.shape[1]), lambda n, m: (m, 0)))
        args.append(a)
    for w, col in w_list:
        in_specs.append(pl.BlockSpec((w.shape[0], tn), lambda n, m, col=col: (0, col(n))))
        args.append(w)
    for x, shape, idx in x_list:
        in_specs.append(pl.BlockSpec(shape, lambda n, m, idx=idx: idx(m, n)))
        args.append(x)
    wbf = [pltpu.VMEM((w.shape[0], tn), BF16) for w, _ in w_list] if cast_w else []
    return pl.pallas_call(
        functools.partial(_tiled_matmul_kernel, n_a=len(a_list), n_w=len(w_list), n_x=len(x_list), n_o=len(o_list),
                          pairs=pairs, cast_w=cast_w, epilogue=epilogue),
        grid=(n_n, n_m),
        in_specs=in_specs,
        out_specs=[pl.BlockSpec(shape, lambda n, m, idx=idx: idx(m, n)) for _, _, shape, idx in o_list],
        out_shape=[jax.ShapeDtypeStruct(shape, dtype) for shape, dtype, _, _ in o_list],
        scratch_shapes=wbf + list(scratch),
        compiler_params=_params(("arbitrary", "arbitrary")),
        name=name,
    )(*args)


def _proj_epilogue(dots, x_refs, o_refs, scratch, *, kind):
    acc = dots[0]()
    o_ref = o_refs[0]
    if kind == "headnorm":
        g_ref = x_refs[0]
        for hh in range(acc.shape[1] // HEAD_DIM):
            sl = slice(hh * HEAD_DIM, (hh + 1) * HEAD_DIM)
            o_ref[:, sl] = (_rms_scale(acc[:, sl]) * g_ref[:, sl]).astype(o_ref.dtype)
    elif kind == "sigmoid":
        o_ref[...] = _sigmoid(acc).astype(o_ref.dtype)
    else:
        o_ref[...] = acc.astype(o_ref.dtype)


def _proj(u, w, col_blocks, tn, out_dtype, kind, gain=None, tm=1024):
    t, _ = u.shape
    tm = min(tm, t)
    nb = len(col_blocks)

    def wcol(n):
        idx = jnp.int32(col_blocks[0])
        for i in range(1, nb):
            idx = jnp.where(n == i, jnp.int32(col_blocks[i]), idx)
        return idx

    x_list = [(gain, (1, tn), lambda m, n: (0, n))] if kind == "headnorm" else []
    return _tiled_matmul(
        "proj_" + kind, functools.partial(_proj_epilogue, kind=kind), [u], [(w, wcol)], x_list,
        [((t, nb * tn), out_dtype, (tm, tn), lambda m, n: (m, n))],
        tm=tm, tn=tn, n_m=t // tm, n_n=nb, pairs=[(0, 0)], cast_w=True)[0]


def _sb_kernel(q_ref, k_ref, v_ref, u_ref, o_ref, *, tq, tk, n_heads, n_ahead, scale):
    i = pl.program_id(2)
    nd = tq // tk
    heads = range(n_heads)
    hcol = lambda hd: slice(hd * HEAD_DIM, (hd + 1) * HEAD_DIM)
    qs = [q_ref[:, hcol(hd)] for hd in heads]
    upper2 = u_ref[...]
    trow = lax.broadcasted_iota(jnp.int32, (tq, tk), 0)
    scol = lax.broadcasted_iota(jnp.int32, (tq, tk), 1)

    def step(blocks, carries, accs):
        pairs = [(bi, hd) for bi in range(len(blocks)) for hd in heads]
        z_, sp_, lb_, part_, later_ = {}, {}, {}, {}, {}
        for bi, hd in pairs:
            z_[bi, hd] = _dot_nt(qs[hd], k_ref[pl.ds(blocks[bi][0], tk), hcol(hd)]) * (scale * LOG2E)
        for bi, hd in pairs:
            z, mask = z_[bi, hd], blocks[bi][1]
            sp = jnp.maximum(z, 0.0) + jnp.log2(1.0 + jnp.exp2(-jnp.abs(z)))
            lb_[bi, hd] = z - sp
            if mask is not None:
                sp = jnp.where(mask, sp, 0.0)
            sp_hi = sp.astype(BF16)
            sp_[bi, hd] = sp
            part_[bi, hd] = jnp.concatenate([sp_hi, (sp - sp_hi.astype(F32)).astype(BF16)], axis=1)
        for p in pairs:
            later_[p] = _dot(part_[p], upper2)
        carries, accs, before = list(carries), list(accs), {}
        for bi, hd in pairs:
            before[bi, hd] = carries[hd]
            carries[hd] = carries[hd] + (later_[bi, hd][:, :1] + sp_[bi, hd][:, :1])
        least = functools.reduce(jnp.minimum, carries).min()
        for bi, hd in pairs:
            mask = blocks[bi][1]
            w = jnp.exp2(lb_[bi, hd] - later_[bi, hd] - before[bi, hd])
            w = (w if mask is None else jnp.where(mask, w, 0.0)).astype(BF16)
            accs[hd] = accs[hd] + _dot(w, v_ref[pl.ds(blocks[bi][0], tk), hcol(hd)])
        return least, tuple(carries), tuple(accs)

    zero = (tuple(jnp.zeros((tq, 1), F32) for _ in heads), tuple(jnp.zeros((tq, HEAD_DIM), F32) for _ in heads))
    first = i * nd
    diag = [(pl.multiple_of((first + d) * tk, tk), (scol + d * tk) < trow) for d in range(nd - 1, -1, -1)]
    ahead = [(pl.multiple_of((first - 1 - a) * tk, tk), None) for a in range(n_ahead)]
    least, carries, accs = lax.cond(first >= n_ahead, lambda: step(diag + ahead, *zero), lambda: step(diag, *zero))
    start = jnp.where(first >= n_ahead, first - 1 - n_ahead, first - 1)

    def cond(c):
        return jnp.logical_and(c[0] >= 0, c[1] < EXP2_UNDERFLOW)

    def body(c):
        j, _, carries, accs = c
        return (j - 1,) + step([(pl.multiple_of(j * tk, tk), None)], carries, accs)

    _, _, _, accs = lax.while_loop(cond, body, (start, least, carries, accs))
    for hd in heads:
        o_ref[:, hcol(hd)] = accs[hd].astype(o_ref.dtype)


def _sb_attention(qk, vhi, batch, seq, tq=128, tk=128, hp=4, n_ahead=2):
    t = batch * seq
    tq = min(tq, seq)
    nq = seq // tq
    ng = N_HEADS // hp
    w = hp * HEAD_DIM
    tri = np.tril(np.ones((tk, tk), np.float32), -1)
    upper2 = jnp.asarray(np.concatenate([tri, tri], axis=0), dtype=BF16)
    return pl.pallas_call(
        functools.partial(_sb_kernel, tq=tq, tk=tk, n_heads=hp, n_ahead=n_ahead, scale=HEAD_DIM ** -0.5),
        grid=(batch, ng, nq),
        in_specs=[pl.BlockSpec((tq, w), lambda b, h, i: (b * nq + i, h)),
                  pl.BlockSpec((seq, w), lambda b, h, i: (b, ng + h)),
                  pl.BlockSpec((seq, w), lambda b, h, i: (b, h)),
                  pl.BlockSpec((2 * tk, tk), lambda b, h, i: (0, 0))],
        out_specs=pl.BlockSpec((tq, w), lambda b, h, i: (b * nq + i, h)),
        out_shape=jax.ShapeDtypeStruct((t, MIX_WIDTH), BF16),
        compiler_params=_params(("parallel", "parallel", "arbitrary")),
        name="sb_attention",
    )(qk, qk, vhi, upper2)


def _hgrn_cumsum_matrix():
    tri = np.tril(np.ones((HG_CHUNK, HG_CHUNK), np.float32))
    return np.concatenate([tri, tri, tri], axis=1)


def _hgrn_kernel(xq_ref, xf_ref, xo_ref, v_ref, lbl_ref, gout_ref, a_ref, o_ref, st_ref, *, n_chunks, n_heads):
    c_ = HG_CHUNK

    @pl.when(pl.program_id(2) == 0)
    def _():
        st_ref[...] = jnp.zeros_like(st_ref)

    logits = lbl_ref[...]
    e = jnp.exp(logits - jnp.max(logits, axis=0, keepdims=True))
    lb_all = e[0:1, :] / jnp.sum(e, axis=0, keepdims=True)
    a_mat = a_ref[...]
    ti = lax.broadcasted_iota(jnp.int32, (c_, c_), 0)
    si = lax.broadcasted_iota(jnp.int32, (c_, c_), 1)
    tr = lax.broadcasted_iota(jnp.int32, (c_, HEAD_DIM), 0)
    pair_of = {c: ((ti ^ si) < 2 * c) & ((ti & (2 * c - 1)) >= c) & ((si & (2 * c - 1)) < c)
               for c in (32, 16, 8, 4, 2, 1)}
    diag = ti == si
    r4 = tr & 3
    odd = (tr & 1) == 1

    def level_decays(b2, f):
        out = {}
        for c in (32, 16, 8, 4):
            bm = jnp.concatenate(
                [jnp.broadcast_to(b2[s0 + c - 1:s0 + c, :], (2 * c, HEAD_DIM)) for s0 in range(0, c_, 2 * c)], axis=0)
            out[c] = jnp.exp2(-jnp.abs(b2 - bm))
        f_prev = pltpu.roll(f, 1, axis=0)
        f_next = pltpu.roll(f, c_ - 1, axis=0)
        out[2] = jnp.where(r4 == 0, f_next, jnp.where(r4 == 1, 1.0, jnp.where(r4 == 2, f, f_prev * f)))
        out[1] = jnp.where(odd, f, 1.0)
        return out

    pairs = [(ch, hd) for ch in range(n_chunks) for hd in range(n_heads)]
    blk = lambda ch, hd: (slice(ch * c_, (ch + 1) * c_), slice(hd * HEAD_DIM, (hd + 1) * HEAD_DIM))
    f_, kk_, qs_, g3_, b_, s_ = {}, {}, {}, {}, {}, {}
    o_intra, q_in, d_state, e_last = {}, {}, {}, {}
    for p in pairs:
        rows, cols = blk(*p)
        lb = lb_all[:, cols]
        xq = xq_ref[rows, cols]
        f = lb + (1.0 - lb) * _sigmoid(xf_ref[rows, cols])
        g = jnp.log(f)
        g_hi = g.astype(BF16)
        r = g - g_hi.astype(F32)
        g_mid = r.astype(BF16)
        g_lo = (r - g_mid.astype(F32)).astype(BF16)
        f_[p], kk_[p], qs_[p] = f, 1.0 - f, xq * _sigmoid(xq)
        g3_[p] = jnp.concatenate([g_hi, g_mid, g_lo], axis=0)
    for p in pairs:
        b_[p] = _dot(a_mat, g3_[p]) * LOG2E
    for p in pairs:
        qs, kk = qs_[p], kk_[p]
        s = jnp.where(diag, _dot_nt(qs.astype(BF16), kk.astype(BF16)), 0.0)
        for c, decay in level_decays(b_[p], f_[p]).items():
            sl = _dot_nt((qs * decay).astype(BF16), (kk * decay).astype(BF16))
            s = s + jnp.where(pair_of[c], sl, 0.0)
        s_[p] = s.astype(BF16)
    for p in pairs:
        rows, cols = blk(*p)
        v = v_ref[rows, cols]
        b2 = b_[p]
        b2_last = b2[c_ - 1:c_, :]
        o_intra[p] = _dot(s_[p], v)
        q_in[p] = (qs_[p] * jnp.exp2(b2)).astype(BF16)
        d_state[p] = _dot_tn(v, (kk_[p] * jnp.exp2(b2_last - b2)).astype(BF16))
        e_last[p] = jnp.exp2(b2_last)

    for hd in range(n_heads):
        cols = slice(hd * HEAD_DIM, (hd + 1) * HEAD_DIM)
        st = st_ref[hd]
        for ch in range(n_chunks):
            rows = slice(ch * c_, (ch + 1) * c_)
            o = o_intra[ch, hd] + _dot_nt(q_in[ch, hd], st.astype(BF16))
            st = st * e_last[ch, hd] + d_state[ch, hd]
            xo = xo_ref[rows, cols]
            o_ref[rows, cols] = (_rms_scale(o) * gout_ref[...] * (xo * _sigmoid(xo))).astype(o_ref.dtype)
        st_ref[hd] = st


def _hgrn2(hgf, vhi, lb_logits, g_out, batch, seq, tc=512, hp=4):
    t = batch * seq
    tc = min(tc, seq)
    ns = seq // tc
    ng = N_HEADS // hp
    w = hp * HEAD_DIM
    a_mat = jnp.asarray(_hgrn_cumsum_matrix(), dtype=BF16)
    row = lambda off: (lambda b, h, i: (b * ns + i, off + h))
    return pl.pallas_call(
        functools.partial(_hgrn_kernel, n_chunks=tc // HG_CHUNK, n_heads=hp),
        grid=(batch, ng, ns),
        in_specs=[pl.BlockSpec((tc, w), row(0)),
                  pl.BlockSpec((tc, w), row(ng)),
                  pl.BlockSpec((tc, w), row(2 * ng)),
                  pl.BlockSpec((tc, w), row(ng)),
                  pl.BlockSpec((lb_logits.shape[0], w), lambda b, h, i: (0, h)),
                  pl.BlockSpec((1, HEAD_DIM), lambda b, h, i: (0, 0)),
                  pl.BlockSpec(a_mat.shape, lambda b, h, i: (0, 0))],
        out_specs=pl.BlockSpec((tc, w), row(0)),
        out_shape=jax.ShapeDtypeStruct((t, MIX_WIDTH), BF16),
        scratch_shapes=[pltpu.VMEM((hp, HEAD_DIM, HEAD_DIM), F32)],
        compiler_params=_params(("parallel", "parallel", "arbitrary")),
        name="hgrn2",
    )(hgf, hgf, hgf, vhi, lb_logits, g_out.reshape(1, HEAD_DIM), a_mat)


def _merge_epilogue(dots, x_refs, o_refs, scratch):
    ga, gb = x_refs
    ma = ga[...].astype(F32) * dots[0]()
    o_refs[0][...] = (ma + gb[...].astype(F32) * dots[1]()).astype(BF16)


def _merge(ya, yb, pa, pb, gates, tm=1024, tn=1024):
    t, _ = ya.shape
    d = pa.shape[1]
    tm = min(tm, t)
    nn = d // tn
    return _tiled_matmul(
        "merge", _merge_epilogue, [ya, yb], [(pa, lambda n: n), (pb, lambda n: n)],
        [(gates, (tm, tn), lambda m, n: (m, n)), (gates, (tm, tn), lambda m, n: (m, nn + n))],
        [((t, d), BF16, (tm, tn), lambda m, n: (m, n))],
        tm=tm, tn=tn, n_m=t // tm, n_n=nn, pairs=[(0, 0), (1, 1)], cast_w=True)[0]


def _out_proj_epilogue(dots, x_refs, o_refs, scratch):
    x_ref, g_ref = x_refs
    h = x_ref[...] + dots[0]()
    o_refs[0][...] = h
    o_refs[1][...] = (_rms_scale(h) * g_ref[...]).astype(BF16)


def _out_proj(m, w, x, g, tm=512):
    t, d = x.shape
    tm = min(tm, t)
    return _tiled_matmul(
        "out_proj", _out_proj_epilogue, [m], [(w, lambda n: n)],
        [(x, (tm, d), lambda m_, n: (m_, 0)), (g.reshape(1, d), (1, d), lambda m_, n: (0, 0))],
        [((t, d), F32, (tm, d), lambda m_, n: (m_, 0)), ((t, d), BF16, (tm, d), lambda m_, n: (m_, 0))],
        tm=tm, tn=d, n_m=t // tm, n_n=1, pairs=[(0, 0)])


def _ffn_up_epilogue(dots, x_refs, o_refs, scratch, *, tm, tr, tiles_per_seq):
    cwg_ref, cwv_ref, cbg_ref, cbv_ref = x_refs
    eg_ref, ev_ref = scratch

    @pl.when(pl.program_id(1) % tiles_per_seq == 0)
    def _():
        eg_ref[0:CONV_HALO, :] = jnp.zeros((CONV_HALO, eg_ref.shape[1]), F32)
        ev_ref[0:CONV_HALO, :] = jnp.zeros((CONV_HALO, ev_ref.shape[1]), F32)

    def conv(up, cw_ref, cb_ref, e_ref, lo):
        e_ref[CONV_HALO + lo:CONV_HALO + lo + tr, :] = up
        out = cb_ref[...] + cw_ref[0:1, :] * e_ref[CONV_HALO - 2 + lo:CONV_HALO - 2 + lo + tr, :]
        out = out + cw_ref[1:2, :] * e_ref[CONV_HALO - 1 + lo:CONV_HALO - 1 + lo + tr, :]
        return out + cw_ref[2:3, :] * up

    for lo in range(0, tm, tr):
        rows = slice(lo, lo + tr)
        gate = conv(dots[0](rows), cwg_ref, cbg_ref, eg_ref, lo)
        val = conv(dots[1](rows), cwv_ref, cbv_ref, ev_ref, lo)
        o_refs[0][rows, :] = (gate * _sigmoid(gate) * val).astype(BF16)
    eg_ref[0:CONV_HALO, :] = eg_ref[tm:tm + CONV_HALO, :]
    ev_ref[0:CONV_HALO, :] = ev_ref[tm:tm + CONV_HALO, :]


def _ffn_up(hn, w_up, conv_w, conv_b, seq, tm=1024, tn=512, tr=1024):
    t, _ = hn.shape
    dff = w_up.shape[1] // 2
    tm = min(tm, seq)
    nn = dff // tn
    conv_b = conv_b.reshape(1, -1)
    return _tiled_matmul(
        "ffn_up", functools.partial(_ffn_up_epilogue, tm=tm, tr=min(tr, tm), tiles_per_seq=seq // tm),
        [hn], [(w_up, lambda n: n), (w_up, lambda n: nn + n)],
        [(conv_w, (CONV_WIDTH, tn), lambda m, n: (0, n)), (conv_w, (CONV_WIDTH, tn), lambda m, n: (0, nn + n)),
         (conv_b, (1, tn), lambda m, n: (0, n)), (conv_b, (1, tn), lambda m, n: (0, nn + n))],
        [((t, dff), BF16, (tm, tn), lambda m, n: (m, n))],
        tm=tm, tn=tn, n_m=t // tm, n_n=nn, pairs=[(0, 0), (0, 1)], cast_w=True,
        scratch=[pltpu.VMEM((tm + CONV_HALO, tn), F32)] * 2)[0]


def _ffn_down_epilogue(dots, x_refs, o_refs, scratch):
    o_refs[0][...] = x_refs[0][...] + dots[0]()


def _ffn_down(act, w, h, tm=512, tn=1024):
    t, _ = act.shape
    d = w.shape[1]
    tm = min(tm, t)
    return _tiled_matmul(
        "ffn_down", _ffn_down_epilogue, [act], [(w, lambda n: n)],
        [(h, (tm, tn), lambda m, n: (m, n))],
        [((t, d), F32, (tm, tn), lambda m, n: (m, n))],
        tm=tm, tn=tn, n_m=t // tm, n_n=d // tn, pairs=[(0, 0)])[0]


def kernel(x, g_mix, w_in, g_q, g_k, lb_logits, g_hg_out, p_a, p_b, w_o, g_ffn, w_up, conv_w, conv_b, w_down):
    batch, seq, d = x.shape
    depth = g_mix.shape[0]
    assert depth == 1 and lb_logits.shape == (depth + 1, MIX_WIDTH)
    h = x.reshape(batch * seq, d)
    for layer in range(depth):
        u = _rmsnorm(h, g_mix[layer], tm=min(512, seq))
        qk_gain = jnp.concatenate([jnp.tile(g_q[layer], N_HEADS), jnp.tile(g_k[layer], N_HEADS)]).reshape(1, -1)
        qk = _proj(u, w_in[layer], (0, 1), MIX_WIDTH, BF16, "headnorm", gain=qk_gain)
        vhi = _proj(u, w_in[layer], (2, 5), MIX_WIDTH, BF16, "cast")
        hgf = _proj(u, w_in[layer], (3, 4, 6), MIX_WIDTH, F32, "cast")
        gates = _proj(u, w_in[layer], (7, 8, 9, 10), MIX_WIDTH, BF16, "sigmoid")

        y_a = _sb_attention(qk, vhi, batch, seq)
        y_b = _hgrn2(hgf, vhi, lb_logits, g_hg_out[layer], batch, seq)

        m = _merge(y_a, y_b, p_a[layer], p_b[layer], gates)
        h, hn = _out_proj(m, w_o[layer].astype(BF16), h, g_ffn[layer])
        act = _ffn_up(hn, w_up[layer], conv_w[layer], conv_b[layer], seq)
        h = _ffn_down(act, w_down[layer].astype(BF16), h)
    return h.reshape(batch, seq, d)
```

```python
import functools

import numpy as np
import jax
import jax.numpy as jnp
from jax import lax
from jax.experimental import pallas as pl
from jax.experimental.pallas import tpu as pltpu

F32 = jnp.float32
BF16 = jnp.bfloat16

EPS = 1e-6
HEAD_DIM = 128
N_HEADS = 8
MIX_WIDTH = N_HEADS * HEAD_DIM
HG_CHUNK = 64
CONV_WIDTH = 3
CONV_HALO = 8
EXP2_UNDERFLOW = 150.0
LOG2E = 1.4426950408889634
V7X_VMEM_LIMIT = 56 * 1024 * 1024


def _params(semantics, vmem_bytes=V7X_VMEM_LIMIT):
    return pltpu.CompilerParams(dimension_semantics=semantics, vmem_limit_bytes=vmem_bytes)


def _dot(a, b):
    return jnp.dot(a, b, preferred_element_type=F32)


def _dot_nt(a, b):
    return lax.dot_general(a, b, (((1,), (1,)), ((), ())), preferred_element_type=F32)


def _dot_tn(a, b):
    return lax.dot_general(a, b, (((0,), (0,)), ((), ())), preferred_element_type=F32)


def _sigmoid(x):
    return 1.0 / (1.0 + jnp.exp(-x))


def _rms_scale(x):
    return x * lax.rsqrt(jnp.mean(x * x, axis=-1, keepdims=True) + EPS)


def _rmsnorm_kernel(x_ref, g_ref, o_ref):
    o_ref[...] = (_rms_scale(x_ref[...]) * g_ref[...]).astype(o_ref.dtype)


def _rmsnorm(x, g, tm):
    t, d = x.shape
    return pl.pallas_call(
        _rmsnorm_kernel,
        grid=(t // tm,),
        in_specs=[pl.BlockSpec((tm, d), lambda i: (i, 0)),
                  pl.BlockSpec((1, d), lambda i: (0, 0))],
        out_specs=pl.BlockSpec((tm, d), lambda i: (i, 0)),
        out_shape=jax.ShapeDtypeStruct((t, d), BF16),
        compiler_params=_params(("parallel",)),
        name="rmsnorm",
    )(x, g.reshape(1, d))


def _tiled_matmul_kernel(*refs, n_a, n_w, n_x, n_o, pairs, cast_w, epilogue):
    a_refs = refs[:n_a]
    w_refs = refs[n_a:n_a + n_w]
    x_refs = refs[n_a + n_w:n_a + n_w + n_x]
    o_refs = refs[n_a + n_w + n_x:n_a + n_w + n_x + n_o]
    scratch = refs[n_a + n_w + n_x + n_o:]
    if cast_w:
        wbf_refs, scratch = scratch[:n_w], scratch[n_w:]

        @pl.when(pl.program_id(1) == 0)
        def _():
            for w_ref, wbf in zip(w_refs, wbf_refs):
                wbf[...] = w_ref[...].astype(BF16)
        w_refs = wbf_refs

    def dot_of(ai, wi):
        return lambda: _dot(a_refs[ai][...], w_refs[wi][...])

    epilogue([dot_of(ai, wi) for ai, wi in pairs], x_refs, o_refs, scratch)


def _tiled_matmul(name, epilogue, a_list, w_list, x_list, o_list, *, tm, tn, n_m, n_n, pairs, cast_w=False,
                  scratch=()):
    in_specs, args = [], []
    for a in a_list:
        in_specs.append(pl.BlockSpec((tm, a.shape[1]), lambda n, m: (m, 0)))
        args.append(a)
    for w, col in w_list:
        in_specs.append(pl.BlockSpec((w.shape[0], tn), lambda n, m, col=col: (0, col(n))))
        args.append(w)
    for x, shape, idx in x_list:
        in_specs.append(pl.BlockSpec(shape, lambda n, m, idx=idx: idx(m, n)))
        args.append(x)
    wbf = [pltpu.VMEM((w.shape[0], tn), BF16) for w, _ in w_list] if cast_w else []
    return pl.pallas_call(
        functools.partial(_tiled_matmul_kernel, n_a=len(a_list), n_w=len(w_list), n_x=len(x_list), n_o=len(o_list),
                          pairs=pairs, cast_w=cast_w, epilogue=epilogue),
        grid=(n_n, n_m),
        in_specs=in_specs,
        out_specs=[pl.BlockSpec(shape, lambda n, m, idx=idx: idx(m, n)) for _, _, shape, idx in o_list],
        out_shape=[jax.ShapeDtypeStruct(shape, dtype) for shape, dtype, _, _ in o_list],
        scratch_shapes=wbf + list(scratch),
        compiler_params=_params(("arbitrary", "arbitrary")),
        name=name,
    )(*args)


def _proj_epilogue(dots, x_refs, o_refs, scratch, *, kind):
    acc = dots[0]()
    o_ref = o_refs[0]
    if kind == "headnorm":
        g_ref = x_refs[0]
        for hh in range(acc.shape[1] // HEAD_DIM):
            sl = slice(hh * HEAD_DIM, (hh + 1) * HEAD_DIM)
            o_ref[:, sl] = (_rms_scale(acc[:, sl]) * g_ref[:, sl]).astype(o_ref.dtype)
    elif kind == "sigmoid":
        o_ref[...] = _sigmoid(acc).astype(o_ref.dtype)
    else:
        o_ref[...] = acc.astype(o_ref.dtype)


def _proj(u, w, col_blocks, tn, out_dtype, kind, gain=None, tm=1024):
    t, _ = u.shape
    tm = min(tm, t)
    nb = len(col_blocks)

    def wcol(n):
        idx = jnp.int32(col_blocks[0])
        for i in range(1, nb):
            idx = jnp.where(n == i, jnp.int32(col_blocks[i]), idx)
        return idx

    x_list = [(gain, (1, tn), lambda m, n: (0, n))] if kind == "headnorm" else []
    return _tiled_matmul(
        "proj_" + kind, functools.partial(_proj_epilogue, kind=kind), [u], [(w, wcol)], x_list,
        [((t, nb * tn), out_dtype, (tm, tn), lambda m, n: (m, n))],
        tm=tm, tn=tn, n_m=t // tm, n_n=nb, pairs=[(0, 0)], cast_w=True)[0]


def _sb_kernel(q_ref, k_ref, v_ref, u_ref, o_ref, acc_ref, car_ref, *, tq, tk, n_heads, n_ahead, scale):
    i = pl.program_id(2)
    nd = tq // tk
    heads = range(n_heads)
    hcol = lambda hd: slice(hd * HEAD_DIM, (hd + 1) * HEAD_DIM)
    qs = [q_ref[:, hcol(hd)] for hd in heads]
    upper2 = u_ref[...]
    trow = lax.broadcasted_iota(jnp.int32, (tq, tk), 0)
    scol = lax.broadcasted_iota(jnp.int32, (tq, tk), 1)

    def step(blocks, carries, accs):
        pairs = [(bi, hd) for bi in range(len(blocks)) for hd in heads]
        z_, sp_, lb_, part_, later_ = {}, {}, {}, {}, {}
        for bi, hd in pairs:
            z_[bi, hd] = _dot_nt(qs[hd], k_ref[pl.ds(blocks[bi][0], tk), hcol(hd)]) * (scale * LOG2E)
        for bi, hd in pairs:
            z, mask = z_[bi, hd], blocks[bi][1]
            sp = jnp.maximum(z, 0.0) + jnp.log2(1.0 + jnp.exp2(-jnp.abs(z)))
            lb_[bi, hd] = z - sp
            if mask is not None:
                sp = jnp.where(mask, sp, 0.0)
            sp_hi = sp.astype(BF16)
            sp_[bi, hd] = sp
            part_[bi, hd] = jnp.concatenate([sp_hi, (sp - sp_hi.astype(F32)).astype(BF16)], axis=1)
        for p in pairs:
            later_[p] = _dot(part_[p], upper2)
        carries, accs, before = list(carries), list(accs), {}
        for bi, hd in pairs:
            before[bi, hd] = carries[hd]
            carries[hd] = carries[hd] + (later_[bi, hd][:, :1] + sp_[bi, hd][:, :1])
        least = functools.reduce(jnp.minimum, carries).min()
        for bi, hd in pairs:
            mask = blocks[bi][1]
            w = jnp.exp2(lb_[bi, hd] - later_[bi, hd] - before[bi, hd])
            w = (w if mask is None else jnp.where(mask, w, 0.0)).astype(BF16)
            accs[hd] = accs[hd] + _dot(w, v_ref[pl.ds(blocks[bi][0], tk), hcol(hd)])
        return least, tuple(carries), tuple(accs)

    def run(blocks, carries, accs):
        least, carries, accs = step(blocks, carries, accs)
        for hd in heads:
            o_ref[:, hcol(hd)] = accs[hd].astype(o_ref.dtype)
            acc_ref[hd] = accs[hd]
            car_ref[hd] = carries[hd]
        return least

    zero = (tuple(jnp.zeros((tq, 1), F32) for _ in heads), tuple(jnp.zeros((tq, HEAD_DIM), F32) for _ in heads))
    first = i * nd
    diag = [(pl.multiple_of((first + d) * tk, tk), (scol + d * tk) < trow) for d in range(nd - 1, -1, -1)]
    ahead = [(pl.multiple_of((first - 1 - a) * tk, tk), None) for a in range(n_ahead)]
    least = lax.cond(first >= n_ahead, lambda: run(diag + ahead, *zero), lambda: run(diag, *zero))
    start = jnp.where(first >= n_ahead, first - 1 - n_ahead, first - 1)

    def cond(c):
        return jnp.logical_and(c[0] >= 0, c[1] < EXP2_UNDERFLOW)

    def body(c):
        blocks = [(pl.multiple_of(c[0] * tk, tk), None)]
        return c[0] - 1, run(blocks, [car_ref[hd] for hd in heads], [acc_ref[hd] for hd in heads])

    lax.while_loop(cond, body, (start, least))


def _sb_attention(qk, vhi, batch, seq, tq=128, tk=128, hp=4, n_ahead=2):
    t = batch * seq
    tq = min(tq, seq)
    nq = seq // tq
    ng = N_HEADS // hp
    w = hp * HEAD_DIM
    tri = np.tril(np.ones((tk, tk), np.float32), -1)
    upper2 = jnp.asarray(np.concatenate([tri, tri], axis=0), dtype=BF16)
    return pl.pallas_call(
        functools.partial(_sb_kernel, tq=tq, tk=tk, n_heads=hp, n_ahead=n_ahead, scale=HEAD_DIM ** -0.5),
        grid=(batch, ng, nq),
        in_specs=[pl.BlockSpec((tq, w), lambda b, h, i: (b * nq + i, h)),
                  pl.BlockSpec((seq, w), lambda b, h, i: (b, ng + h)),
                  pl.BlockSpec((seq, w), lambda b, h, i: (b, h)),
                  pl.BlockSpec((2 * tk, tk), lambda b, h, i: (0, 0))],
        out_specs=pl.BlockSpec((tq, w), lambda b, h, i: (b * nq + i, h)),
        out_shape=jax.ShapeDtypeStruct((t, MIX_WIDTH), BF16),
        scratch_shapes=[pltpu.VMEM((hp, tq, HEAD_DIM), F32), pltpu.VMEM((hp, tq, 1), F32)],
        compiler_params=_params(("parallel", "parallel", "arbitrary")),
        name="sb_attention",
    )(qk, qk, vhi, upper2)


def _hgrn_cumsum_matrix():
    tri = np.tril(np.ones((HG_CHUNK, HG_CHUNK), np.float32))
    return np.concatenate([tri, tri, tri], axis=1)


def _hgrn_kernel(xq_ref, xf_ref, xo_ref, v_ref, lbl_ref, gout_ref, a_ref, o_ref, st_ref, *, n_chunks, n_heads):
    c_ = HG_CHUNK

    @pl.when(pl.program_id(2) == 0)
    def _():
        st_ref[...] = jnp.zeros_like(st_ref)

    logits = lbl_ref[...]
    e = jnp.exp(logits - jnp.max(logits, axis=0, keepdims=True))
    lb_all = e[0:1, :] / jnp.sum(e, axis=0, keepdims=True)
    a_mat = a_ref[...]
    ti = lax.broadcasted_iota(jnp.int32, (c_, c_), 0)
    si = lax.broadcasted_iota(jnp.int32, (c_, c_), 1)
    tr = lax.broadcasted_iota(jnp.int32, (c_, HEAD_DIM), 0)
    pair_of = {c: ((ti ^ si) < 2 * c) & ((ti & (2 * c - 1)) >= c) & ((si & (2 * c - 1)) < c)
               for c in (32, 16, 8, 4, 2, 1)}
    diag = ti == si
    r4 = tr & 3
    odd = (tr & 1) == 1

    def level_decays(b2, f):
        out = {}
        for c in (32, 16, 8, 4):
            bm = jnp.concatenate(
                [jnp.broadcast_to(b2[s0 + c - 1:s0 + c, :], (2 * c, HEAD_DIM)) for s0 in range(0, c_, 2 * c)], axis=0)
            out[c] = jnp.exp2(-jnp.abs(b2 - bm))
        f_prev = pltpu.roll(f, 1, axis=0)
        f_next = pltpu.roll(f, c_ - 1, axis=0)
        out[2] = jnp.where(r4 == 0, f_next, jnp.where(r4 == 1, 1.0, jnp.where(r4 == 2, f, f_prev * f)))
        out[1] = jnp.where(odd, f, 1.0)
        return out

    pairs = [(ch, hd) for ch in range(n_chunks) for hd in range(n_heads)]
    blk = lambda ch, hd: (slice(ch * c_, (ch + 1) * c_), slice(hd * HEAD_DIM, (hd + 1) * HEAD_DIM))
    f_, kk_, qs_, g3_, b_, s_ = {}, {}, {}, {}, {}, {}
    o_intra, q_in, d_state, e_last = {}, {}, {}, {}
    for p in pairs:
        rows, cols = blk(*p)
        lb = lb_all[:, cols]
        xq = xq_ref[rows, cols]
        f = lb + (1.0 - lb) * _sigmoid(xf_ref[rows, cols])
        g = jnp.log(f)
        g_hi = g.astype(BF16)
        r = g - g_hi.astype(F32)
        g_mid = r.astype(BF16)
        g_lo = (r - g_mid.astype(F32)).astype(BF16)
        f_[p], kk_[p], qs_[p] = f, 1.0 - f, xq * _sigmoid(xq)
        g3_[p] = jnp.concatenate([g_hi, g_mid, g_lo], axis=0)
    for p in pairs:
        b_[p] = _dot(a_mat, g3_[p]) * LOG2E
    for p in pairs:
        qs, kk = qs_[p], kk_[p]
        s = jnp.where(diag, _dot_nt(qs.astype(BF16), kk.astype(BF16)), 0.0)
        for c, decay in level_decays(b_[p], f_[p]).items():
            s = jnp.where(pair_of[c], _dot_nt((qs * decay).astype(BF16), (kk * decay).astype(BF16)), s)
        s_[p] = s.astype(BF16)
    for p in pairs:
        rows, cols = blk(*p)
        v = v_ref[rows, cols]
        b2 = b_[p]
        b2_last = b2[c_ - 1:c_, :]
        o_intra[p] = _dot(s_[p], v)
        q_in[p] = (qs_[p] * jnp.exp2(b2)).astype(BF16)
        d_state[p] = _dot_tn(v, (kk_[p] * jnp.exp2(b2_last - b2)).astype(BF16))
        e_last[p] = jnp.exp2(b2_last)

    for hd in range(n_heads):
        cols = slice(hd * HEAD_DIM, (hd + 1) * HEAD_DIM)
        st = st_ref[hd]
        for ch in range(n_chunks):
            rows = slice(ch * c_, (ch + 1) * c_)
            o = o_intra[ch, hd] + _dot_nt(q_in[ch, hd], st.astype(BF16))
            st = st * e_last[ch, hd] + d_state[ch, hd]
            xo = xo_ref[rows, cols]
            o_ref[rows, cols] = (_rms_scale(o) * gout_ref[...] * (xo * _sigmoid(xo))).astype(o_ref.dtype)
        st_ref[hd] = st


def _hgrn2(hgf, vhi, lb_logits, g_out, batch, seq, tc=512, hp=4):
    t = batch * seq
    tc = min(tc, seq)
    ns = seq // tc
    ng = N_HEADS // hp
    w = hp * HEAD_DIM
    a_mat = jnp.asarray(_hgrn_cumsum_matrix(), dtype=BF16)
    row = lambda off: (lambda b, h, i: (b * ns + i, off + h))
    return pl.pallas_call(
        functools.partial(_hgrn_kernel, n_chunks=tc // HG_CHUNK, n_heads=hp),
        grid=(batch, ng, ns),
        in_specs=[pl.BlockSpec((tc, w), row(0)),
                  pl.BlockSpec((tc, w), row(ng)),
                  pl.BlockSpec((tc, w), row(2 * ng)),
                  pl.BlockSpec((tc, w), row(ng)),
                  pl.BlockSpec((lb_logits.shape[0], w), lambda b, h, i: (0, h)),
                  pl.BlockSpec((1, HEAD_DIM), lambda b, h, i: (0, 0)),
                  pl.BlockSpec(a_mat.shape, lambda b, h, i: (0, 0))],
        out_specs=pl.BlockSpec((tc, w), row(0)),
        out_shape=jax.ShapeDtypeStruct((t, MIX_WIDTH), BF16),
        scratch_shapes=[pltpu.VMEM((hp, HEAD_DIM, HEAD_DIM), F32)],
        compiler_params=_params(("parallel", "parallel", "arbitrary")),
        name="hgrn2",
    )(hgf, hgf, hgf, vhi, lb_logits, g_out.reshape(1, HEAD_DIM), a_mat)


def _merge_epilogue(dots, x_refs, o_refs, scratch):
    ga, gb = x_refs
    ma = ga[...].astype(F32) * dots[0]()
    o_refs[0][...] = (ma + gb[...].astype(F32) * dots[1]()).astype(BF16)


def _merge(ya, yb, pa, pb, gates, tm=1024, tn=1024):
    t, _ = ya.shape
    d = pa.shape[1]
    tm = min(tm, t)
    nn = d // tn
    return _tiled_matmul(
        "merge", _merge_epilogue, [ya, yb], [(pa, lambda n: n), (pb, lambda n: n)],
        [(gates, (tm, tn), lambda m, n: (m, n)), (gates, (tm, tn), lambda m, n: (m, nn + n))],
        [((t, d), BF16, (tm, tn), lambda m, n: (m, n))],
        tm=tm, tn=tn, n_m=t // tm, n_n=nn, pairs=[(0, 0), (1, 1)], cast_w=True)[0]


def _out_proj_epilogue(dots, x_refs, o_refs, scratch):
    x_ref, g_ref = x_refs
    h = x_ref[...] + dots[0]()
    o_refs[0][...] = h
    o_refs[1][...] = (_rms_scale(h) * g_ref[...]).astype(BF16)


def _out_proj(m, w, x, g, tm=512):
    t, d = x.shape
    tm = min(tm, t)
    return _tiled_matmul(
        "out_proj", _out_proj_epilogue, [m], [(w, lambda n: n)],
        [(x, (tm, d), lambda m_, n: (m_, 0)), (g.reshape(1, d), (1, d), lambda m_, n: (0, 0))],
        [((t, d), F32, (tm, d), lambda m_, n: (m_, 0)), ((t, d), BF16, (tm, d), lambda m_, n: (m_, 0))],
        tm=tm, tn=d, n_m=t // tm, n_n=1, pairs=[(0, 0)])


def _ffn_up_epilogue(dots, x_refs, o_refs, scratch, *, tm, tiles_per_seq):
    cwg_ref, cwv_ref, cbg_ref, cbv_ref = x_refs
    hg_ref, hv_ref = scratch

    @pl.when(pl.program_id(1) % tiles_per_seq == 0)
    def _():
        hg_ref[...] = jnp.zeros_like(hg_ref)
        hv_ref[...] = jnp.zeros_like(hv_ref)

    def conv(up, halo, cw_ref, cb_ref):
        edge = jnp.concatenate([halo, up[0:CONV_HALO, :]], axis=0)
        out = cb_ref[...] + cw_ref[2:3, :] * up
        for j in (1, 2):
            shifted = jnp.concatenate([edge[CONV_HALO - j:2 * CONV_HALO - j, :],
                                       pltpu.roll(up, j, axis=0)[CONV_HALO:, :]], axis=0)
            out = out + cw_ref[2 - j:3 - j, :] * shifted
        return out

    up_g = dots[0]()
    gate = conv(up_g, hg_ref[...], cwg_ref, cbg_ref)
    up_v = dots[1]()
    val = conv(up_v, hv_ref[...], cwv_ref, cbv_ref)
    o_refs[0][...] = (gate * _sigmoid(gate) * val).astype(BF16)
    hg_ref[...] = up_g[tm - CONV_HALO:, :]
    hv_ref[...] = up_v[tm - CONV_HALO:, :]


def _ffn_up(hn, w_up, conv_w, conv_b, seq, tm=1024, tn=512):
    t, _ = hn.shape
    dff = w_up.shape[1] // 2
    tm = min(tm, seq)
    nn = dff // tn
    conv_b = conv_b.reshape(1, -1)
    return _tiled_matmul(
        "ffn_up", functools.partial(_ffn_up_epilogue, tm=tm, tiles_per_seq=seq // tm),
        [hn], [(w_up, lambda n: n), (w_up, lambda n: nn + n)],
        [(conv_w, (CONV_WIDTH, tn), lambda m, n: (0, n)), (conv_w, (CONV_WIDTH, tn), lambda m, n: (0, nn + n)),
         (conv_b, (1, tn), lambda m, n: (0, n)), (conv_b, (1, tn), lambda m, n: (0, nn + n))],
        [((t, dff), BF16, (tm, tn), lambda m, n: (m, n))],
        tm=tm, tn=tn, n_m=t // tm, n_n=nn, pairs=[(0, 0), (0, 1)], cast_w=True,
        scratch=[pltpu.VMEM((CONV_HALO, tn), F32)] * 2)[0]


def _ffn_down_epilogue(dots, x_refs, o_refs, scratch):
    o_refs[0][...] = x_refs[0][...] + dots[0]()


def _ffn_down(act, w, h, tm=512, tn=1024):
    t, _ = act.shape
    d = w.shape[1]
    tm = min(tm, t)
    return _tiled_matmul(
        "ffn_down", _ffn_down_epilogue, [act], [(w, lambda n: n)],
        [(h, (tm, tn), lambda m, n: (m, n))],
        [((t, d), F32, (tm, tn), lambda m, n: (m, n))],
        tm=tm, tn=tn, n_m=t // tm, n_n=d // tn, pairs=[(0, 0)])[0]


def kernel(x, g_mix, w_in, g_q, g_k, lb_logits, g_hg_out, p_a, p_b, w_o, g_ffn, w_up, conv_w, conv_b, w_down):
    batch, seq, d = x.shape
    depth = g_mix.shape[0]
    assert depth == 1 and lb_logits.shape == (depth + 1, MIX_WIDTH)
    h = x.reshape(batch * seq, d)
    for layer in range(depth):
        u = _rmsnorm(h, g_mix[layer], tm=min(512, seq))
        qk_gain = jnp.concatenate([jnp.tile(g_q[layer], N_HEADS), jnp.tile(g_k[layer], N_HEADS)]).reshape(1, -1)
        qk = _proj(u, w_in[layer], (0, 1), MIX_WIDTH, BF16, "headnorm", gain=qk_gain)
        vhi = _proj(u, w_in[layer], (2, 5), MIX_WIDTH, BF16, "cast")
        hgf = _proj(u, w_in[layer], (3, 4, 6), MIX_WIDTH, F32, "cast")
        gates = _proj(u, w_in[layer], (7, 8, 9, 10), MIX_WIDTH, BF16, "sigmoid")

        y_a = _sb_attention(qk, vhi, batch, seq)
        y_b = _hgrn2(hgf, vhi, lb_logits, g_hg_out[layer], batch, seq)

        m = _merge(y_a, y_b, p_a[layer], p_b[layer], gates)
        h, hn = _out_proj(m, w_o[layer].astype(BF16), h, g_ffn[layer])
        act = _ffn_up(hn, w_up[layer], conv_w[layer], conv_b[layer], seq)
        h = _ffn_down(act, w_down[layer].astype(BF16), h)
    return h.reshape(batch, seq, d)
```

```python
import functools

import numpy as np
import jax
import jax.numpy as jnp
from jax import lax
from jax.experimental import pallas as pl
from jax.experimental.pallas import tpu as pltpu

F32 = jnp.float32
BF16 = jnp.bfloat16

EPS = 1e-6
HEAD_DIM = 128
N_HEADS = 8
MIX_WIDTH = N_HEADS * HEAD_DIM
HG_CHUNK = 64
CONV_WIDTH = 3
CONV_HALO = 8
EXP2_UNDERFLOW = 150.0
LOG2E = 1.4426950408889634
V7X_VMEM_LIMIT = 56 * 1024 * 1024


def _params(semantics, vmem_bytes=V7X_VMEM_LIMIT):
    return pltpu.CompilerParams(dimension_semantics=semantics, vmem_limit_bytes=vmem_bytes)


def _dot(a, b):
    return jnp.dot(a, b, preferred_element_type=F32)


def _dot_nt(a, b):
    return lax.dot_general(a, b, (((1,), (1,)), ((), ())), preferred_element_type=F32)


def _dot_tn(a, b):
    return lax.dot_general(a, b, (((0,), (0,)), ((), ())), preferred_element_type=F32)


def _sigmoid(x):
    return 1.0 / (1.0 + jnp.exp(-x))


def _rms_scale(x):
    return x * lax.rsqrt(jnp.mean(x * x, axis=-1, keepdims=True) + EPS)


def _rmsnorm_kernel(x_ref, g_ref, o_ref):
    o_ref[...] = (_rms_scale(x_ref[...]) * g_ref[...]).astype(o_ref.dtype)


def _rmsnorm(x, g, tm):
    t, d = x.shape
    return pl.pallas_call(
        _rmsnorm_kernel,
        grid=(t // tm,),
        in_specs=[pl.BlockSpec((tm, d), lambda i: (i, 0)),
                  pl.BlockSpec((1, d), lambda i: (0, 0))],
        out_specs=pl.BlockSpec((tm, d), lambda i: (i, 0)),
        out_shape=jax.ShapeDtypeStruct((t, d), BF16),
        compiler_params=_params(("parallel",)),
        name="rmsnorm",
    )(x, g.reshape(1, d))


def _tiled_matmul_kernel(*refs, n_a, n_w, n_x, n_o, pairs, cast_w, epilogue):
    a_refs = refs[:n_a]
    w_refs = refs[n_a:n_a + n_w]
    x_refs = refs[n_a + n_w:n_a + n_w + n_x]
    o_refs = refs[n_a + n_w + n_x:n_a + n_w + n_x + n_o]
    scratch = refs[n_a + n_w + n_x + n_o:]
    if cast_w:
        wbf_refs, scratch = scratch[:n_w], scratch[n_w:]

        @pl.when(pl.program_id(1) == 0)
        def _():
            for w_ref, wbf in zip(w_refs, wbf_refs):
                wbf[...] = w_ref[...].astype(BF16)
        w_refs = wbf_refs

    def dot_of(ai, wi):
        return lambda: _dot(a_refs[ai][...], w_refs[wi][...])

    epilogue([dot_of(ai, wi) for ai, wi in pairs], x_refs, o_refs, scratch)


def _tiled_matmul(name, epilogue, a_list, w_list, x_list, o_list, *, tm, tn, n_m, n_n, pairs, cast_w=False,
                  scratch=()):
    in_specs, args = [], []
    for a in a_list:
        in_specs.append(pl.BlockSpec((tm, a.shape[1]), lambda n, m: (m, 0)))
        args.append(a)
    for w, col in w_list:
        in_specs.append(pl.BlockSpec((w.shape[0], tn), lambda n, m, col=col: (0, col(n))))
        args.append(w)
    for x, shape, idx in x_list:
        in_specs.append(pl.BlockSpec(shape, lambda n, m, idx=idx: idx(m, n)))
        args.append(x)
    wbf = [pltpu.VMEM((w.shape[0], tn), BF16) for w, _ in w_list] if cast_w else []
    return pl.pallas_call(
        functools.partial(_tiled_matmul_kernel, n_a=len(a_list), n_w=len(w_list), n_x=len(x_list), n_o=len(o_list),
                          pairs=pairs, cast_w=cast_w, epilogue=epilogue),
        grid=(n_n, n_m),
        in_specs=in_specs,
        out_specs=[pl.BlockSpec(shape, lambda n, m, idx=idx: idx(m, n)) for _, _, shape, idx in o_list],
        out_shape=[jax.ShapeDtypeStruct(shape, dtype) for shape, dtype, _, _ in o_list],
        scratch_shapes=wbf + list(scratch),
        compiler_params=_params(("arbitrary", "arbitrary")),
        name=name,
    )(*args)


def _proj_epilogue(dots, x_refs, o_refs, scratch, *, kind):
    acc = dots[0]()
    o_ref = o_refs[0]
    if kind == "headnorm":
        g_ref = x_refs[0]
        for hh in range(acc.shape[1] // HEAD_DIM):
            sl = slice(hh * HEAD_DIM, (hh + 1) * HEAD_DIM)
            o_ref[:, sl] = (_rms_scale(acc[:, sl]) * g_ref[:, sl]).astype(o_ref.dtype)
    elif kind == "sigmoid":
        o_ref[...] = _sigmoid(acc).astype(o_ref.dtype)
    else:
        o_ref[...] = acc.astype(o_ref.dtype)


def _proj(u, w, col_blocks, tn, out_dtype, kind, gain=None, tm=1024):
    t, _ = u.shape
    tm = min(tm, t)
    nb = len(col_blocks)

    def wcol(n):
        idx = jnp.int32(col_blocks[0])
        for i in range(1, nb):
            idx = jnp.where(n == i, jnp.int32(col_blocks[i]), idx)
        return idx

    x_list = [(gain, (1, tn), lambda m, n: (0, n))] if kind == "headnorm" else []
    return _tiled_matmul(
        "proj_" + kind, functools.partial(_proj_epilogue, kind=kind), [u], [(w, wcol)], x_list,
        [((t, nb * tn), out_dtype, (tm, tn), lambda m, n: (m, n))],
        tm=tm, tn=tn, n_m=t // tm, n_n=nb, pairs=[(0, 0)], cast_w=True)[0]


def _sb_kernel(q_ref, k_ref, v_ref, u_ref, o_ref, acc_ref, car_ref, *, tq, n_tiles, n_heads, n_ahead, scale):
    i = pl.program_id(2)
    tk = tq
    tiles, heads = range(n_tiles), range(n_heads)
    hcol = lambda hd: slice(hd * HEAD_DIM, (hd + 1) * HEAD_DIM)
    qrow = lambda ti: slice(ti * tq, (ti + 1) * tq)
    upper2 = u_ref[...]
    causal = lax.broadcasted_iota(jnp.int32, (tq, tk), 1) < lax.broadcasted_iota(jnp.int32, (tq, tk), 0)

    def step(work, carries, accs):
        pairs = [(wi, hd) for wi in range(len(work)) for hd in heads]
        z_, sp_, lb_, part_, later_, before = {}, {}, {}, {}, {}, {}
        for wi, hd in pairs:
            ti, off, _ = work[wi]
            z_[wi, hd] = _dot_nt(q_ref[qrow(ti), hcol(hd)], k_ref[pl.ds(off, tk), hcol(hd)]) * (scale * LOG2E)
        for wi, hd in pairs:
            z = z_[wi, hd]
            sp = jnp.maximum(z, 0.0) + jnp.log2(1.0 + jnp.exp2(-jnp.abs(z)))
            lb_[wi, hd] = z - sp
            if work[wi][2]:
                sp = jnp.where(causal, sp, 0.0)
            sp_hi = sp.astype(BF16)
            sp_[wi, hd] = sp
            part_[wi, hd] = jnp.concatenate([sp_hi, (sp - sp_hi.astype(F32)).astype(BF16)], axis=1)
        for p in pairs:
            later_[p] = _dot(part_[p], upper2)
        carries, accs = dict(carries), dict(accs)
        for wi, hd in pairs:
            ch = (work[wi][0], hd)
            before[wi, hd] = carries[ch]
            carries[ch] = carries[ch] + (later_[wi, hd][:, :1] + sp_[wi, hd][:, :1])
        least = tuple(functools.reduce(jnp.minimum, [carries[ti, hd] for hd in heads]).min()
                      for ti in sorted({w[0] for w in work}))
        for wi, hd in pairs:
            ti, off, on_diag = work[wi]
            w = jnp.exp2(lb_[wi, hd] - later_[wi, hd] - before[wi, hd])
            w = (jnp.where(causal, w, 0.0) if on_diag else w).astype(BF16)
            accs[ti, hd] = accs[ti, hd] + _dot(w, v_ref[pl.ds(off, tk), hcol(hd)])
        return least, carries, accs

    def run(work, carries, accs):
        least, carries, accs = step(work, carries, accs)
        for ti, hd in carries:
            o_ref[qrow(ti), hcol(hd)] = accs[ti, hd].astype(o_ref.dtype)
            acc_ref[ti, hd] = accs[ti, hd]
            car_ref[ti, hd] = carries[ti, hd]
        return least

    chains = [(ti, hd) for ti in tiles for hd in heads]
    zero = ({ch: jnp.zeros((tq, 1), F32) for ch in chains}, {ch: jnp.zeros((tq, HEAD_DIM), F32) for ch in chains})
    first = [i * n_tiles + ti for ti in tiles]
    on_diag = [(ti, pl.multiple_of(first[ti] * tk, tk), True) for ti in tiles]
    ahead = [(ti, pl.multiple_of((first[ti] - 1 - a) * tk, tk), False) for ti in tiles for a in range(n_ahead)]
    usual = first[0] >= n_ahead
    least = lax.cond(usual, lambda: run(on_diag + ahead, *zero), lambda: run(on_diag, *zero))

    def cond(c):
        return jnp.logical_and(c[0] >= 0, c[1] < EXP2_UNDERFLOW)

    for ti in tiles:
        def body(c, ti=ti):
            work = [(ti, pl.multiple_of(c[0] * tk, tk), False)]
            return (c[0] - 1,) + run(work, {(ti, hd): car_ref[ti, hd] for hd in heads},
                                     {(ti, hd): acc_ref[ti, hd] for hd in heads})

        lax.while_loop(cond, body, (first[ti] - 1 - jnp.where(usual, n_ahead, 0), least[ti]))


def _sb_attention(qk, vhi, batch, seq, tq=128, n_tiles=4, hp=4, n_ahead=2):
    t = batch * seq
    rows = n_tiles * tq
    nq = seq // rows
    ng = N_HEADS // hp
    w = hp * HEAD_DIM
    tri = np.tril(np.ones((tq, tq), np.float32), -1)
    upper2 = jnp.asarray(np.concatenate([tri, tri], axis=0), dtype=BF16)
    return pl.pallas_call(
        functools.partial(_sb_kernel, tq=tq, n_tiles=n_tiles, n_heads=hp, n_ahead=n_ahead, scale=HEAD_DIM ** -0.5),
        grid=(batch, ng, nq),
        in_specs=[pl.BlockSpec((rows, w), lambda b, h, i: (b * nq + i, h)),
                  pl.BlockSpec((seq, w), lambda b, h, i: (b, ng + h)),
                  pl.BlockSpec((seq, w), lambda b, h, i: (b, h)),
                  pl.BlockSpec((2 * tq, tq), lambda b, h, i: (0, 0))],
        out_specs=pl.BlockSpec((rows, w), lambda b, h, i: (b * nq + i, h)),
        out_shape=jax.ShapeDtypeStruct((t, MIX_WIDTH), BF16),
        scratch_shapes=[pltpu.VMEM((n_tiles, hp, tq, HEAD_DIM), F32), pltpu.VMEM((n_tiles, hp, tq, 1), F32)],
        compiler_params=_params(("parallel", "parallel", "arbitrary")),
        name="sb_attention",
    )(qk, qk, vhi, upper2)


def _hgrn_cumsum_matrix():
    tri = np.tril(np.ones((HG_CHUNK, HG_CHUNK), np.float32))
    return np.concatenate([tri, tri, tri], axis=1)


def _hgrn_kernel(xq_ref, xf_ref, xo_ref, v_ref, lbl_ref, gout_ref, a_ref, o_ref, st_ref, *, n_chunks, n_heads):
    c_ = HG_CHUNK

    @pl.when(pl.program_id(2) == 0)
    def _():
        st_ref[...] = jnp.zeros_like(st_ref)

    logits = lbl_ref[...]
    e = jnp.exp(logits - jnp.max(logits, axis=0, keepdims=True))
    lb_all = e[0:1, :] / jnp.sum(e, axis=0, keepdims=True)
    a_mat = a_ref[...]
    ti = lax.broadcasted_iota(jnp.int32, (c_, c_), 0)
    si = lax.broadcasted_iota(jnp.int32, (c_, c_), 1)
    tr = lax.broadcasted_iota(jnp.int32, (c_, HEAD_DIM), 0)
    pair_of = {c: ((ti ^ si) < 2 * c) & ((ti & (2 * c - 1)) >= c) & ((si & (2 * c - 1)) < c)
               for c in (32, 16, 8, 4, 2, 1)}
    diag = ti == si
    r4 = tr & 3
    odd = (tr & 1) == 1

    def level_decays(b2, f):
        out = {}
        for c in (32, 16, 8, 4):
            bm = jnp.concatenate(
                [jnp.broadcast_to(b2[s0 + c - 1:s0 + c, :], (2 * c, HEAD_DIM)) for s0 in range(0, c_, 2 * c)], axis=0)
            out[c] = jnp.exp2(-jnp.abs(b2 - bm))
        f_prev = pltpu.roll(f, 1, axis=0)
        f_next = pltpu.roll(f, c_ - 1, axis=0)
        out[2] = jnp.where(r4 == 0, f_next, jnp.where(r4 == 1, 1.0, jnp.where(r4 == 2, f, f_prev * f)))
        out[1] = jnp.where(odd, f, 1.0)
        return out

    pairs = [(ch, hd) for ch in range(n_chunks) for hd in range(n_heads)]
    blk = lambda ch, hd: (slice(ch * c_, (ch + 1) * c_), slice(hd * HEAD_DIM, (hd + 1) * HEAD_DIM))
    f_, kk_, qs_, g3_, b_, s_ = {}, {}, {}, {}, {}, {}
    o_intra, q_in, d_state, e_last = {}, {}, {}, {}
    for p in pairs:
        rows, cols = blk(*p)
        lb = lb_all[:, cols]
        xq = xq_ref[rows, cols]
        f = lb + (1.0 - lb) * _sigmoid(xf_ref[rows, cols])
        g = jnp.log(f)
        g_hi = g.astype(BF16)
        r = g - g_hi.astype(F32)
        g_mid = r.astype(BF16)
        g_lo = (r - g_mid.astype(F32)).astype(BF16)
        f_[p], kk_[p], qs_[p] = f, 1.0 - f, xq * _sigmoid(xq)
        g3_[p] = jnp.concatenate([g_hi, g_mid, g_lo], axis=0)
    for p in pairs:
        b_[p] = _dot(a_mat, g3_[p]) * LOG2E
    for p in pairs:
        qs, kk = qs_[p], kk_[p]
        s = jnp.where(diag, _dot_nt(qs.astype(BF16), kk.astype(BF16)), 0.0)
        for c, decay in level_decays(b_[p], f_[p]).items():
            s = jnp.where(pair_of[c], _dot_nt((qs * decay).astype(BF16), (kk * decay).astype(BF16)), s)
        s_[p] = s.astype(BF16)
    for p in pairs:
        rows, cols = blk(*p)
        v = v_ref[rows, cols]
        b2 = b_[p]
        b2_last = b2[c_ - 1:c_, :]
        o_intra[p] = _dot(s_[p], v)
        q_in[p] = (qs_[p] * jnp.exp2(b2)).astype(BF16)
        d_state[p] = _dot_tn(v, (kk_[p] * jnp.exp2(b2_last - b2)).astype(BF16))
        e_last[p] = jnp.exp2(b2_last)

    for hd in range(n_heads):
        cols = slice(hd * HEAD_DIM, (hd + 1) * HEAD_DIM)
        st = st_ref[hd]
        for ch in range(n_chunks):
            rows = slice(ch * c_, (ch + 1) * c_)
            o = o_intra[ch, hd] + _dot_nt(q_in[ch, hd], st.astype(BF16))
            st = st * e_last[ch, hd] + d_state[ch, hd]
            xo = xo_ref[rows, cols]
            o_ref[rows, cols] = (_rms_scale(o) * gout_ref[...] * (xo * _sigmoid(xo))).astype(o_ref.dtype)
        st_ref[hd] = st


def _hgrn2(hgf, vhi, lb_logits, g_out, batch, seq, tc=512, hp=4):
    t = batch * seq
    tc = min(tc, seq)
    ns = seq // tc
    ng = N_HEADS // hp
    w = hp * HEAD_DIM
    a_mat = jnp.asarray(_hgrn_cumsum_matrix(), dtype=BF16)
    row = lambda off: (lambda b, h, i: (b * ns + i, off + h))
    return pl.pallas_call(
        functools.partial(_hgrn_kernel, n_chunks=tc // HG_CHUNK, n_heads=hp),
        grid=(batch, ng, ns),
        in_specs=[pl.BlockSpec((tc, w), row(0)),
                  pl.BlockSpec((tc, w), row(ng)),
                  pl.BlockSpec((tc, w), row(2 * ng)),
                  pl.BlockSpec((tc, w), row(ng)),
                  pl.BlockSpec((lb_logits.shape[0], w), lambda b, h, i: (0, h)),
                  pl.BlockSpec((1, HEAD_DIM), lambda b, h, i: (0, 0)),
                  pl.BlockSpec(a_mat.shape, lambda b, h, i: (0, 0))],
        out_specs=pl.BlockSpec((tc, w), row(0)),
        out_shape=jax.ShapeDtypeStruct((t, MIX_WIDTH), BF16),
        scratch_shapes=[pltpu.VMEM((hp, HEAD_DIM, HEAD_DIM), F32)],
        compiler_params=_params(("parallel", "parallel", "arbitrary")),
        name="hgrn2",
    )(hgf, hgf, hgf, vhi, lb_logits, g_out.reshape(1, HEAD_DIM), a_mat)


def _merge_epilogue(dots, x_refs, o_refs, scratch):
    ga, gb = x_refs
    ma = ga[...].astype(F32) * dots[0]()
    o_refs[0][...] = (ma + gb[...].astype(F32) * dots[1]()).astype(BF16)


def _merge(ya, yb, pa, pb, gates, tm=1024, tn=1024):
    t, _ = ya.shape
    d = pa.shape[1]
    tm = min(tm, t)
    nn = d // tn
    return _tiled_matmul(
        "merge", _merge_epilogue, [ya, yb], [(pa, lambda n: n), (pb, lambda n: n)],
        [(gates, (tm, tn), lambda m, n: (m, n)), (gates, (tm, tn), lambda m, n: (m, nn + n))],
        [((t, d), BF16, (tm, tn), lambda m, n: (m, n))],
        tm=tm, tn=tn, n_m=t // tm, n_n=nn, pairs=[(0, 0), (1, 1)], cast_w=True)[0]


def _out_proj_epilogue(dots, x_refs, o_refs, scratch):
    x_ref, g_ref = x_refs
    h = x_ref[...] + dots[0]()
    o_refs[0][...] = h
    o_refs[1][...] = (_rms_scale(h) * g_ref[...]).astype(BF16)


def _out_proj(m, w, x, g, tm=512):
    t, d = x.shape
    tm = min(tm, t)
    return _tiled_matmul(
        "out_proj", _out_proj_epilogue, [m], [(w, lambda n: n)],
        [(x, (tm, d), lambda m_, n: (m_, 0)), (g.reshape(1, d), (1, d), lambda m_, n: (0, 0))],
        [((t, d), F32, (tm, d), lambda m_, n: (m_, 0)), ((t, d), BF16, (tm, d), lambda m_, n: (m_, 0))],
        tm=tm, tn=d, n_m=t // tm, n_n=1, pairs=[(0, 0)])


def _ffn_up_epilogue(dots, x_refs, o_refs, scratch, *, tm, tiles_per_seq):
    cwg_ref, cwv_ref, cbg_ref, cbv_ref = x_refs
    hg_ref, hv_ref = scratch

    @pl.when(pl.program_id(1) % tiles_per_seq == 0)
    def _():
        hg_ref[...] = jnp.zeros_like(hg_ref)
        hv_ref[...] = jnp.zeros_like(hv_ref)

    def conv(up, halo, cw_ref, cb_ref):
        edge = jnp.concatenate([halo, up[0:CONV_HALO, :]], axis=0)
        out = cb_ref[...] + cw_ref[2:3, :] * up
        for j in (1, 2):
            shifted = jnp.concatenate([edge[CONV_HALO - j:2 * CONV_HALO - j, :],
                                       pltpu.roll(up, j, axis=0)[CONV_HALO:, :]], axis=0)
            out = out + cw_ref[2 - j:3 - j, :] * shifted
        return out

    up_g = dots[0]()
    gate = conv(up_g, hg_ref[...], cwg_ref, cbg_ref)
    up_v = dots[1]()
    val = conv(up_v, hv_ref[...], cwv_ref, cbv_ref)
    o_refs[0][...] = (gate * _sigmoid(gate) * val).astype(BF16)
    hg_ref[...] = up_g[tm - CONV_HALO:, :]
    hv_ref[...] = up_v[tm - CONV_HALO:, :]


def _ffn_up(hn, w_up, conv_w, conv_b, seq, tm=1024, tn=512):
    t, _ = hn.shape
    dff = w_up.shape[1] // 2
    tm = min(tm, seq)
    nn = dff // tn
    conv_b = conv_b.reshape(1, -1)
    return _tiled_matmul(
        "ffn_up", functools.partial(_ffn_up_epilogue, tm=tm, tiles_per_seq=seq // tm),
        [hn], [(w_up, lambda n: n), (w_up, lambda n: nn + n)],
        [(conv_w, (CONV_WIDTH, tn), lambda m, n: (0, n)), (conv_w, (CONV_WIDTH, tn), lambda m, n: (0, nn + n)),
         (conv_b, (1, tn), lambda m, n: (0, n)), (conv_b, (1, tn), lambda m, n: (0, nn + n))],
        [((t, dff), BF16, (tm, tn), lambda m, n: (m, n))],
        tm=tm, tn=tn, n_m=t // tm, n_n=nn, pairs=[(0, 0), (0, 1)], cast_w=True,
        scratch=[pltpu.VMEM((CONV_HALO, tn), F32)] * 2)[0]


def _ffn_down_epilogue(dots, x_refs, o_refs, scratch):
    o_refs[0][...] = x_refs[0][...] + dots[0]()


def _ffn_down(act, w, h, tm=512, tn=1024):
    t, _ = act.shape
    d = w.shape[1]
    tm = min(tm, t)
    return _tiled_matmul(
        "ffn_down", _ffn_down_epilogue, [act], [(w, lambda n: n)],
        [(h, (tm, tn), lambda m, n: (m, n))],
        [((t, d), F32, (tm, tn), lambda m, n: (m, n))],
        tm=tm, tn=tn, n_m=t // tm, n_n=d // tn, pairs=[(0, 0)])[0]


def kernel(x, g_mix, w_in, g_q, g_k, lb_logits, g_hg_out, p_a, p_b, w_o, g_ffn, w_up, conv_w, conv_b, w_down):
    batch, seq, d = x.shape
    depth = g_mix.shape[0]
    assert depth == 1 and lb_logits.shape == (depth + 1, MIX_WIDTH)
    h = x.reshape(batch * seq, d)
    for layer in range(depth):
        u = _rmsnorm(h, g_mix[layer], tm=min(512, seq))
        qk_gain = jnp.concatenate([jnp.tile(g_q[layer], N_HEADS), jnp.tile(g_k[layer], N_HEADS)]).reshape(1, -1)
        qk = _proj(u, w_in[layer], (0, 1), MIX_WIDTH, BF16, "headnorm", gain=qk_gain)
        vhi = _proj(u, w_in[layer], (2, 5), MIX_WIDTH, BF16, "cast")
        hgf = _proj(u, w_in[layer], (3, 4, 6), MIX_WIDTH, F32, "cast")
        gates = _proj(u, w_in[layer], (7, 8, 9, 10), MIX_WIDTH, BF16, "sigmoid")

        y_a = _sb_attention(qk, vhi, batch, seq)
        y_b = _hgrn2(hgf, vhi, lb_logits, g_hg_out[layer], batch, seq)

        m = _merge(y_a, y_b, p_a[layer], p_b[layer], gates)
        h, hn = _out_proj(m, w_o[layer].astype(BF16), h, g_ffn[layer])
        act = _ffn_up(hn, w_up[layer], conv_w[layer], conv_b[layer], seq)
        h = _ffn_down(act, w_down[layer].astype(BF16), h)
    return h.reshape(batch, seq, d)
```

```python
import functools

import numpy as np
import jax
import jax.numpy as jnp
from jax import lax
from jax.experimental import pallas as pl
from jax.experimental.pallas import tpu as pltpu

F32 = jnp.float32
BF16 = jnp.bfloat16

EPS = 1e-6
HEAD_DIM = 128
N_HEADS = 8
MIX_WIDTH = N_HEADS * HEAD_DIM
HG_CHUNK = 64
CONV_WIDTH = 3
CONV_HALO = 8
EXP2_UNDERFLOW = 150.0
LOG2E = 1.4426950408889634
V7X_VMEM_LIMIT = 56 * 1024 * 1024


def _params(semantics, vmem_bytes=V7X_VMEM_LIMIT):
    return pltpu.CompilerParams(dimension_semantics=semantics, vmem_limit_bytes=vmem_bytes)


def _dot(a, b):
    return jnp.dot(a, b, preferred_element_type=F32)


def _dot_nt(a, b):
    return lax.dot_general(a, b, (((1,), (1,)), ((), ())), preferred_element_type=F32)


def _dot_tn(a, b):
    return lax.dot_general(a, b, (((0,), (0,)), ((), ())), preferred_element_type=F32)


def _sigmoid(x):
    return 1.0 / (1.0 + jnp.exp(-x))


def _rms_scale(x):
    return x * lax.rsqrt(jnp.mean(x * x, axis=-1, keepdims=True) + EPS)


def _rmsnorm_kernel(x_ref, g_ref, o_ref):
    o_ref[...] = (_rms_scale(x_ref[...]) * g_ref[...]).astype(o_ref.dtype)


def _rmsnorm(x, g, tm):
    t, d = x.shape
    return pl.pallas_call(
        _rmsnorm_kernel,
        grid=(t // tm,),
        in_specs=[pl.BlockSpec((tm, d), lambda i: (i, 0)),
                  pl.BlockSpec((1, d), lambda i: (0, 0))],
        out_specs=pl.BlockSpec((tm, d), lambda i: (i, 0)),
        out_shape=jax.ShapeDtypeStruct((t, d), BF16),
        compiler_params=_params(("parallel",)),
        name="rmsnorm",
    )(x, g.reshape(1, d))


def _tiled_matmul_kernel(*refs, n_a, n_w, n_x, n_o, pairs, cast_w, epilogue):
    a_refs = refs[:n_a]
    w_refs = refs[n_a:n_a + n_w]
    x_refs = refs[n_a + n_w:n_a + n_w + n_x]
    o_refs = refs[n_a + n_w + n_x:n_a + n_w + n_x + n_o]
    scratch = refs[n_a + n_w + n_x + n_o:]
    if cast_w:
        wbf_refs, scratch = scratch[:n_w], scratch[n_w:]

        @pl.when(pl.program_id(1) == 0)
        def _():
            for w_ref, wbf in zip(w_refs, wbf_refs):
                wbf[...] = w_ref[...].astype(BF16)
        w_refs = wbf_refs

    def dot_of(ai, wi):
        return lambda: _dot(a_refs[ai][...], w_refs[wi][...])

    epilogue([dot_of(ai, wi) for ai, wi in pairs], x_refs, o_refs, scratch)


def _tiled_matmul(name, epilogue, a_list, w_list, x_list, o_list, *, tm, tn, n_m, n_n, pairs, cast_w=False,
                  scratch=()):
    in_specs, args = [], []
    for a in a_list:
        in_specs.append(pl.BlockSpec((tm, a.shape[1]), lambda n, m: (m, 0)))
        args.append(a)
    for w, col in w_list:
        in_specs.append(pl.BlockSpec((w.shape[0], tn), lambda n, m, col=col: (0, col(n))))
        args.append(w)
    for x, shape, idx in x_list:
        in_specs.append(pl.BlockSpec(shape, lambda n, m, idx=idx: idx(m, n)))
        args.append(x)
    wbf = [pltpu.VMEM((w.shape[0], tn), BF16) for w, _ in w_list] if cast_w else []
    return pl.pallas_call(
        functools.partial(_tiled_matmul_kernel, n_a=len(a_list), n_w=len(w_list), n_x=len(x_list), n_o=len(o_list),
                          pairs=pairs, cast_w=cast_w, epilogue=epilogue),
        grid=(n_n, n_m),
        in_specs=in_specs,
        out_specs=[pl.BlockSpec(shape, lambda n, m, idx=idx: idx(m, n)) for _, _, shape, idx in o_list],
        out_shape=[jax.ShapeDtypeStruct(shape, dtype) for shape, dtype, _, _ in o_list],
        scratch_shapes=wbf + list(scratch),
        compiler_params=_params(("arbitrary", "arbitrary")),
        name=name,
    )(*args)


def _proj_epilogue(dots, x_refs, o_refs, scratch, *, kind):
    acc = dots[0]()
    o_ref = o_refs[0]
    if kind == "headnorm":
        g_ref = x_refs[0]
        for hh in range(acc.shape[1] // HEAD_DIM):
            sl = slice(hh * HEAD_DIM, (hh + 1) * HEAD_DIM)
            o_ref[:, sl] = (_rms_scale(acc[:, sl]) * g_ref[:, sl]).astype(o_ref.dtype)
    elif kind == "sigmoid":
        o_ref[...] = _sigmoid(acc).astype(o_ref.dtype)
    else:
        o_ref[...] = acc.astype(o_ref.dtype)


def _proj(u, w, col_blocks, tn, out_dtype, kind, gain=None, tm=1024):
    t, _ = u.shape
    tm = min(tm, t)
    nb = len(col_blocks)

    def wcol(n):
        idx = jnp.int32(col_blocks[0])
        for i in range(1, nb):
            idx = jnp.where(n == i, jnp.int32(col_blocks[i]), idx)
        return idx

    x_list = [(gain, (1, tn), lambda m, n: (0, n))] if kind == "headnorm" else []
    return _tiled_matmul(
        "proj_" + kind, functools.partial(_proj_epilogue, kind=kind), [u], [(w, wcol)], x_list,
        [((t, nb * tn), out_dtype, (tm, tn), lambda m, n: (m, n))],
        tm=tm, tn=tn, n_m=t // tm, n_n=nb, pairs=[(0, 0)], cast_w=True)[0]


def _sb_kernel(q_ref, k_ref, v_ref, u_ref, o_ref, acc_ref, car_ref, *, tq, n_tiles, n_heads, n_ahead, scale):
    i = pl.program_id(2)
    tk = tq
    tiles, heads = range(n_tiles), range(n_heads)
    hcol = lambda hd: slice(hd * HEAD_DIM, (hd + 1) * HEAD_DIM)
    qrow = lambda ti: slice(ti * tq, (ti + 1) * tq)
    upper2 = u_ref[...]
    causal = lax.broadcasted_iota(jnp.int32, (tq, tk), 1) < lax.broadcasted_iota(jnp.int32, (tq, tk), 0)

    def step(work, carries, accs):
        pairs = [(wi, hd) for wi in range(len(work)) for hd in heads]
        z_, sp_, lb_, part_, later_, before = {}, {}, {}, {}, {}, {}
        for wi, hd in pairs:
            ti, off, _ = work[wi]
            z_[wi, hd] = _dot_nt(q_ref[qrow(ti), hcol(hd)], k_ref[pl.ds(off, tk), hcol(hd)]) * (scale * LOG2E)
        for wi, hd in pairs:
            z = z_[wi, hd]
            sp = jnp.maximum(z, 0.0) + jnp.log2(1.0 + jnp.exp2(-jnp.abs(z)))
            lb_[wi, hd] = z - sp
            if work[wi][2]:
                sp = jnp.where(causal, sp, 0.0)
            sp_hi = sp.astype(BF16)
            sp_[wi, hd] = sp
            part_[wi, hd] = jnp.concatenate([sp_hi, (sp - sp_hi.astype(F32)).astype(BF16)], axis=1)
        for p in pairs:
            later_[p] = _dot(part_[p], upper2)
        carries, accs = dict(carries), dict(accs)
        for wi, hd in pairs:
            ch = (work[wi][0], hd)
            before[wi, hd] = carries[ch]
            carries[ch] = carries[ch] + (later_[wi, hd][:, :1] + sp_[wi, hd][:, :1])
        least = tuple(functools.reduce(jnp.minimum, [carries[ti, hd] for hd in heads]).min()
                      for ti in sorted({w[0] for w in work}))
        for wi, hd in pairs:
            ti, off, on_diag = work[wi]
            w = jnp.exp2(lb_[wi, hd] - later_[wi, hd] - before[wi, hd])
            w = (jnp.where(causal, w, 0.0) if on_diag else w).astype(BF16)
            accs[ti, hd] = accs[ti, hd] + _dot(w, v_ref[pl.ds(off, tk), hcol(hd)])
        return least, carries, accs

    def run(work, carries, accs):
        least, carries, accs = step(work, carries, accs)
        for ti, hd in carries:
            o_ref[qrow(ti), hcol(hd)] = accs[ti, hd].astype(o_ref.dtype)
            acc_ref[ti, hd] = accs[ti, hd]
            car_ref[ti, hd] = carries[ti, hd]
        return least

    chains = [(ti, hd) for ti in tiles for hd in heads]
    zero = ({ch: jnp.zeros((tq, 1), F32) for ch in chains}, {ch: jnp.zeros((tq, HEAD_DIM), F32) for ch in chains})
    first = [i * n_tiles + ti for ti in tiles]
    on_diag = [(ti, pl.multiple_of(first[ti] * tk, tk), True) for ti in tiles]
    ahead = [(ti, pl.multiple_of((first[ti] - 1 - a) * tk, tk), False) for ti in tiles for a in range(n_ahead)]
    usual = first[0] >= n_ahead
    least = lax.cond(usual, lambda: run(on_diag + ahead, *zero), lambda: run(on_diag, *zero))

    def cond(c):
        return jnp.logical_and(c[0] >= 0, c[1] < EXP2_UNDERFLOW)

    for ti in tiles:
        def body(c, ti=ti):
            work = [(ti, pl.multiple_of(c[0] * tk, tk), False)]
            return (c[0] - 1,) + run(work, {(ti, hd): car_ref[ti, hd] for hd in heads},
                                     {(ti, hd): acc_ref[ti, hd] for hd in heads})

        lax.while_loop(cond, body, (first[ti] - 1 - jnp.where(usual, n_ahead, 0), least[ti]))


def _sb_attention(qk, vhi, batch, seq, tq=128, n_tiles=4, hp=4, n_ahead=2):
    t = batch * seq
    rows = n_tiles * tq
    nq = seq // rows
    ng = N_HEADS // hp
    w = hp * HEAD_DIM
    tri = np.tril(np.ones((tq, tq), np.float32), -1)
    upper2 = jnp.asarray(np.concatenate([tri, tri], axis=0), dtype=BF16)
    return pl.pallas_call(
        functools.partial(_sb_kernel, tq=tq, n_tiles=n_tiles, n_heads=hp, n_ahead=n_ahead, scale=HEAD_DIM ** -0.5),
        grid=(batch, ng, nq),
        in_specs=[pl.BlockSpec((rows, w), lambda b, h, i: (b * nq + i, h)),
                  pl.BlockSpec((seq, w), lambda b, h, i: (b, ng + h)),
                  pl.BlockSpec((seq, w), lambda b, h, i: (b, h)),
                  pl.BlockSpec((2 * tq, tq), lambda b, h, i: (0, 0))],
        out_specs=pl.BlockSpec((rows, w), lambda b, h, i: (b * nq + i, h)),
        out_shape=jax.ShapeDtypeStruct((t, MIX_WIDTH), BF16),
        scratch_shapes=[pltpu.VMEM((n_tiles, hp, tq, HEAD_DIM), F32), pltpu.VMEM((n_tiles, hp, tq, 1), F32)],
        compiler_params=_params(("parallel", "parallel", "arbitrary")),
        name="sb_attention",
    )(qk, qk, vhi, upper2)


def _hgrn_cumsum_matrix():
    tri = np.tril(np.ones((HG_CHUNK, HG_CHUNK), np.float32))
    return np.concatenate([tri, tri, tri], axis=1)


def _hgrn_kernel(xq_ref, xf_ref, xo_ref, v_ref, lbl_ref, gout_ref, a_ref, o_ref, st_ref, *, n_chunks, n_heads):
    c_ = HG_CHUNK

    @pl.when(pl.program_id(2) == 0)
    def _():
        st_ref[...] = jnp.zeros_like(st_ref)

    logits = lbl_ref[...]
    e = jnp.exp(logits - jnp.max(logits, axis=0, keepdims=True))
    lb_all = e[0:1, :] / jnp.sum(e, axis=0, keepdims=True)
    a_mat = a_ref[...]
    ti = lax.broadcasted_iota(jnp.int32, (c_, c_), 0)
    si = lax.broadcasted_iota(jnp.int32, (c_, c_), 1)
    tr = lax.broadcasted_iota(jnp.int32, (c_, HEAD_DIM), 0)
    pair_of = {c: ((ti ^ si) < 2 * c) & ((ti & (2 * c - 1)) >= c) & ((si & (2 * c - 1)) < c)
               for c in (32, 16, 8, 4, 2, 1)}
    diag = ti == si
    r4 = tr & 3
    odd = (tr & 1) == 1

    def level_decays(b2, f):
        out = {}
        for c in (32, 16, 8, 4):
            bm = jnp.concatenate(
                [jnp.broadcast_to(b2[s0 + c - 1:s0 + c, :], (2 * c, HEAD_DIM)) for s0 in range(0, c_, 2 * c)], axis=0)
            out[c] = jnp.exp2(-jnp.abs(b2 - bm))
        f_prev = pltpu.roll(f, 1, axis=0)
        f_next = pltpu.roll(f, c_ - 1, axis=0)
        out[2] = jnp.where(r4 == 0, f_next, jnp.where(r4 == 1, 1.0, jnp.where(r4 == 2, f, f_prev * f)))
        out[1] = jnp.where(odd, f, 1.0)
        return out

    pairs = [(ch, hd) for ch in range(n_chunks) for hd in range(n_heads)]
    blk = lambda ch, hd: (slice(ch * c_, (ch + 1) * c_), slice(hd * HEAD_DIM, (hd + 1) * HEAD_DIM))
    f_, kk_, qs_, g3_, b_, s_ = {}, {}, {}, {}, {}, {}
    o_intra, q_in, d_state, e_last = {}, {}, {}, {}
    for p in pairs:
        rows, cols = blk(*p)
        lb = lb_all[:, cols]
        xq = xq_ref[rows, cols]
        f = lb + (1.0 - lb) * _sigmoid(xf_ref[rows, cols])
        g = jnp.log(f)
        g_hi = g.astype(BF16)
        r = g - g_hi.astype(F32)
        g_mid = r.astype(BF16)
        g_lo = (r - g_mid.astype(F32)).astype(BF16)
        f_[p], kk_[p], qs_[p] = f, 1.0 - f, xq * _sigmoid(xq)
        g3_[p] = jnp.concatenate([g_hi, g_mid, g_lo], axis=0)
    for p in pairs:
        b_[p] = _dot(a_mat, g3_[p]) * LOG2E
    for p in pairs:
        qs, kk = qs_[p].astype(BF16), kk_[p].astype(BF16)
        s = jnp.where(diag, _dot_nt(qs, kk), 0.0)
        for c, decay in level_decays(b_[p], f_[p]).items():
            decay = decay.astype(BF16)
            s = jnp.where(pair_of[c], _dot_nt(qs * decay, kk * decay), s)
        s_[p] = s.astype(BF16)
    for p in pairs:
        rows, cols = blk(*p)
        v = v_ref[rows, cols]
        b2 = b_[p]
        b2_last = b2[c_ - 1:c_, :]
        o_intra[p] = _dot(s_[p], v)
        q_in[p] = qs_[p].astype(BF16) * jnp.exp2(b2).astype(BF16)
        d_state[p] = _dot_tn(v, kk_[p].astype(BF16) * jnp.exp2(b2_last - b2).astype(BF16))
        e_last[p] = jnp.exp2(b2_last)

    for hd in range(n_heads):
        cols = slice(hd * HEAD_DIM, (hd + 1) * HEAD_DIM)
        st = st_ref[hd]
        for ch in range(n_chunks):
            rows = slice(ch * c_, (ch + 1) * c_)
            o = o_intra[ch, hd] + _dot_nt(q_in[ch, hd], st.astype(BF16))
            st = st * e_last[ch, hd] + d_state[ch, hd]
            xo = xo_ref[rows, cols]
            o_ref[rows, cols] = (_rms_scale(o) * gout_ref[...] * (xo * _sigmoid(xo))).astype(o_ref.dtype)
        st_ref[hd] = st


def _hgrn2(hgf, vhi, lb_logits, g_out, batch, seq, tc=512, hp=4):
    t = batch * seq
    tc = min(tc, seq)
    ns = seq // tc
    ng = N_HEADS // hp
    w = hp * HEAD_DIM
    a_mat = jnp.asarray(_hgrn_cumsum_matrix(), dtype=BF16)
    row = lambda off: (lambda b, h, i: (b * ns + i, off + h))
    return pl.pallas_call(
        functools.partial(_hgrn_kernel, n_chunks=tc // HG_CHUNK, n_heads=hp),
        grid=(batch, ng, ns),
        in_specs=[pl.BlockSpec((tc, w), row(0)),
                  pl.BlockSpec((tc, w), row(ng)),
                  pl.BlockSpec((tc, w), row(2 * ng)),
                  pl.BlockSpec((tc, w), row(ng)),
                  pl.BlockSpec((lb_logits.shape[0], w), lambda b, h, i: (0, h)),
                  pl.BlockSpec((1, HEAD_DIM), lambda b, h, i: (0, 0)),
                  pl.BlockSpec(a_mat.shape, lambda b, h, i: (0, 0))],
        out_specs=pl.BlockSpec((tc, w), row(0)),
        out_shape=jax.ShapeDtypeStruct((t, MIX_WIDTH), BF16),
        scratch_shapes=[pltpu.VMEM((hp, HEAD_DIM, HEAD_DIM), F32)],
        compiler_params=_params(("parallel", "parallel", "arbitrary")),
        name="hgrn2",
    )(hgf, hgf, hgf, vhi, lb_logits, g_out.reshape(1, HEAD_DIM), a_mat)


def _merge_epilogue(dots, x_refs, o_refs, scratch):
    ga, gb = x_refs
    ma = ga[...].astype(F32) * dots[0]()
    o_refs[0][...] = (ma + gb[...].astype(F32) * dots[1]()).astype(BF16)


def _merge(ya, yb, pa, pb, gates, tm=1024, tn=1024):
    t, _ = ya.shape
    d = pa.shape[1]
    tm = min(tm, t)
    nn = d // tn
    return _tiled_matmul(
        "merge", _merge_epilogue, [ya, yb], [(pa, lambda n: n), (pb, lambda n: n)],
        [(gates, (tm, tn), lambda m, n: (m, n)), (gates, (tm, tn), lambda m, n: (m, nn + n))],
        [((t, d), BF16, (tm, tn), lambda m, n: (m, n))],
        tm=tm, tn=tn, n_m=t // tm, n_n=nn, pairs=[(0, 0), (1, 1)], cast_w=True)[0]


def _out_proj_epilogue(dots, x_refs, o_refs, scratch):
    x_ref, g_ref = x_refs
    h = x_ref[...] + dots[0]()
    o_refs[0][...] = h
    o_refs[1][...] = (_rms_scale(h) * g_ref[...]).astype(BF16)


def _out_proj(m, w, x, g, tm=512):
    t, d = x.shape
    tm = min(tm, t)
    return _tiled_matmul(
        "out_proj", _out_proj_epilogue, [m], [(w, lambda n: n)],
        [(x, (tm, d), lambda m_, n: (m_, 0)), (g.reshape(1, d), (1, d), lambda m_, n: (0, 0))],
        [((t, d), F32, (tm, d), lambda m_, n: (m_, 0)), ((t, d), BF16, (tm, d), lambda m_, n: (m_, 0))],
        tm=tm, tn=d, n_m=t // tm, n_n=1, pairs=[(0, 0)])


def _ffn_up_epilogue(dots, x_refs, o_refs, scratch, *, tm, tiles_per_seq):
    cwg_ref, cwv_ref, cbg_ref, cbv_ref = x_refs
    hg_ref, hv_ref = scratch

    @pl.when(pl.program_id(1) % tiles_per_seq == 0)
    def _():
        hg_ref[...] = jnp.zeros_like(hg_ref)
        hv_ref[...] = jnp.zeros_like(hv_ref)

    def conv(up, halo, cw_ref, cb_ref):
        edge = jnp.concatenate([halo, up[0:CONV_HALO, :]], axis=0)
        out = cb_ref[...] + cw_ref[2:3, :] * up
        for j in (1, 2):
            shifted = jnp.concatenate([edge[CONV_HALO - j:2 * CONV_HALO - j, :],
                                       pltpu.roll(up, j, axis=0)[CONV_HALO:, :]], axis=0)
            out = out + cw_ref[2 - j:3 - j, :] * shifted
        return out

    up_g = dots[0]()
    gate = conv(up_g, hg_ref[...], cwg_ref, cbg_ref)
    up_v = dots[1]()
    val = conv(up_v, hv_ref[...], cwv_ref, cbv_ref)
    o_refs[0][...] = (gate * _sigmoid(gate) * val).astype(BF16)
    hg_ref[...] = up_g[tm - CONV_HALO:, :]
    hv_ref[...] = up_v[tm - CONV_HALO:, :]


def _ffn_up(hn, w_up, conv_w, conv_b, seq, tm=1024, tn=512):
    t, _ = hn.shape
    dff = w_up.shape[1] // 2
    tm = min(tm, seq)
    nn = dff // tn
    conv_b = conv_b.reshape(1, -1)
    return _tiled_matmul(
        "ffn_up", functools.partial(_ffn_up_epilogue, tm=tm, tiles_per_seq=seq // tm),
        [hn], [(w_up, lambda n: n), (w_up, lambda n: nn + n)],
        [(conv_w, (CONV_WIDTH, tn), lambda m, n: (0, n)), (conv_w, (CONV_WIDTH, tn), lambda m, n: (0, nn + n)),
         (conv_b, (1, tn), lambda m, n: (0, n)), (conv_b, (1, tn), lambda m, n: (0, nn + n))],
        [((t, dff), BF16, (tm, tn), lambda m, n: (m, n))],
        tm=tm, tn=tn, n_m=t // tm, n_n=nn, pairs=[(0, 0), (0, 1)], cast_w=True,
        scratch=[pltpu.VMEM((CONV_HALO, tn), F32)] * 2)[0]


def _ffn_down_epilogue(dots, x_refs, o_refs, scratch):
    o_refs[0][...] = x_refs[0][...] + dots[0]()


def _ffn_down(act, w, h, tm=512, tn=1024):
    t, _ = act.shape
    d = w.shape[1]
    tm = min(tm, t)
    return _tiled_matmul(
        "ffn_down", _ffn_down_epilogue, [act], [(w, lambda n: n)],
        [(h, (tm, tn), lambda m, n: (m, n))],
        [((t, d), F32, (tm, tn), lambda m, n: (m, n))],
        tm=tm, tn=tn, n_m=t // tm, n_n=d // tn, pairs=[(0, 0)])[0]


def kernel(x, g_mix, w_in, g_q, g_k, lb_logits, g_hg_out, p_a, p_b, w_o, g_ffn, w_up, conv_w, conv_b, w_down):
    batch, seq, d = x.shape
    depth = g_mix.shape[0]
    assert depth == 1 and lb_logits.shape == (depth + 1, MIX_WIDTH)
    h = x.reshape(batch * seq, d)
    for layer in range(depth):
        u = _rmsnorm(h, g_mix[layer], tm=min(512, seq))
        qk_gain = jnp.concatenate([jnp.tile(g_q[layer], N_HEADS), jnp.tile(g_k[layer], N_HEADS)]).reshape(1, -1)
        qk = _proj(u, w_in[layer], (0, 1), MIX_WIDTH, BF16, "headnorm", gain=qk_gain)
        vhi = _proj(u, w_in[layer], (2, 5), MIX_WIDTH, BF16, "cast")
        hgf = _proj(u, w_in[layer], (3, 4, 6), MIX_WIDTH, F32, "cast")
        gates = _proj(u, w_in[layer], (7, 8, 9, 10), MIX_WIDTH, BF16, "sigmoid")

        y_a = _sb_attention(qk, vhi, batch, seq)
        y_b = _hgrn2(hgf, vhi, lb_logits, g_hg_out[layer], batch, seq)

        m = _merge(y_a, y_b, p_a[layer], p_b[layer], gates)
        h, hn = _out_proj(m, w_o[layer].astype(BF16), h, g_ffn[layer])
        act = _ffn_up(hn, w_up[layer], conv_w[layer], conv_b[layer], seq)
        h = _ffn_down(act, w_down[layer].astype(BF16), h)
    return h.reshape(batch, seq, d)
```

```python
import functools

import numpy as np
import jax
import jax.numpy as jnp
from jax import lax
from jax.experimental import pallas as pl
from jax.experimental.pallas import tpu as pltpu

F32 = jnp.float32
BF16 = jnp.bfloat16

EPS = 1e-6
HEAD_DIM = 128
N_HEADS = 8
MIX_WIDTH = N_HEADS * HEAD_DIM
HG_CHUNK = 64
CONV_WIDTH = 3
CONV_HALO = 8
EXP2_UNDERFLOW = 150.0
LOG2E = 1.4426950408889634
V7X_VMEM_LIMIT = 56 * 1024 * 1024


def _params(semantics, vmem_bytes=V7X_VMEM_LIMIT):
    return pltpu.CompilerParams(dimension_semantics=semantics, vmem_limit_bytes=vmem_bytes)


def _dot(a, b):
    return jnp.dot(a, b, preferred_element_type=F32)


def _dot_nt(a, b):
    return lax.dot_general(a, b, (((1,), (1,)), ((), ())), preferred_element_type=F32)


def _dot_tn(a, b):
    return lax.dot_general(a, b, (((0,), (0,)), ((), ())), preferred_element_type=F32)


def _sigmoid(x):
    return 1.0 / (1.0 + jnp.exp(-x))


def _rms_scale(x):
    return x * lax.rsqrt(jnp.mean(x * x, axis=-1, keepdims=True) + EPS)


def _rmsnorm_kernel(x_ref, g_ref, o_ref):
    o_ref[...] = (_rms_scale(x_ref[...]) * g_ref[...]).astype(o_ref.dtype)


def _rmsnorm(x, g, tm):
    t, d = x.shape
    return pl.pallas_call(
        _rmsnorm_kernel,
        grid=(t // tm,),
        in_specs=[pl.BlockSpec((tm, d), lambda i: (i, 0)),
                  pl.BlockSpec((1, d), lambda i: (0, 0))],
        out_specs=pl.BlockSpec((tm, d), lambda i: (i, 0)),
        out_shape=jax.ShapeDtypeStruct((t, d), BF16),
        compiler_params=_params(("parallel",)),
        name="rmsnorm",
    )(x, g.reshape(1, d))


def _tiled_matmul_kernel(*refs, n_a, n_w, n_x, n_o, pairs, cast_w, epilogue):
    a_refs = refs[:n_a]
    w_refs = refs[n_a:n_a + n_w]
    x_refs = refs[n_a + n_w:n_a + n_w + n_x]
    o_refs = refs[n_a + n_w + n_x:n_a + n_w + n_x + n_o]
    scratch = refs[n_a + n_w + n_x + n_o:]
    if cast_w:
        wbf_refs, scratch = scratch[:n_w], scratch[n_w:]

        @pl.when(pl.program_id(1) == 0)
        def _():
            for w_ref, wbf in zip(w_refs, wbf_refs):
                wbf[...] = w_ref[...].astype(BF16)
        w_refs = wbf_refs

    def dot_of(ai, wi):
        return lambda: _dot(a_refs[ai][...], w_refs[wi][...])

    epilogue([dot_of(ai, wi) for ai, wi in pairs], x_refs, o_refs, scratch)


def _tiled_matmul(name, epilogue, a_list, w_list, x_list, o_list, *, tm, tn, n_m, n_n, pairs, cast_w=False,
                  scratch=()):
    in_specs, args = [], []
    for a in a_list:
        in_specs.append(pl.BlockSpec((tm, a.shape[1]), lambda n, m: (m, 0)))
        args.append(a)
    for w, col in w_list:
        in_specs.append(pl.BlockSpec((w.shape[0], tn), lambda n, m, col=col: (0, col(n))))
        args.append(w)
    for x, shape, idx in x_list:
        in_specs.append(pl.BlockSpec(shape, lambda n, m, idx=idx: idx(m, n)))
        args.append(x)
    wbf = [pltpu.VMEM((w.shape[0], tn), BF16) for w, _ in w_list] if cast_w else []
    return pl.pallas_call(
        functools.partial(_tiled_matmul_kernel, n_a=len(a_list), n_w=len(w_list), n_x=len(x_list), n_o=len(o_list),
                          pairs=pairs, cast_w=cast_w, epilogue=epilogue),
        grid=(n_n, n_m),
        in_specs=in_specs,
        out_specs=[pl.BlockSpec(shape, lambda n, m, idx=idx: idx(m, n)) for _, _, shape, idx in o_list],
        out_shape=[jax.ShapeDtypeStruct(shape, dtype) for shape, dtype, _, _ in o_list],
        scratch_shapes=wbf + list(scratch),
        compiler_params=_params(("arbitrary", "arbitrary")),
        name=name,
    )(*args)


def _proj_epilogue(dots, x_refs, o_refs, scratch, *, kind):
    acc = dots[0]()
    o_ref = o_refs[0]
    if kind == "headnorm":
        g_ref = x_refs[0]
        for hh in range(acc.shape[1] // HEAD_DIM):
            sl = slice(hh * HEAD_DIM, (hh + 1) * HEAD_DIM)
            o_ref[:, sl] = (_rms_scale(acc[:, sl]) * g_ref[:, sl]).astype(o_ref.dtype)
    elif kind == "sigmoid":
        o_ref[...] = _sigmoid(acc).astype(o_ref.dtype)
    else:
        o_ref[...] = acc.astype(o_ref.dtype)


def _proj(u, w, col_blocks, tn, out_dtype, kind, gain=None, tm=1024):
    t, _ = u.shape
    tm = min(tm, t)
    nb = len(col_blocks)

    def wcol(n):
        idx = jnp.int32(col_blocks[0])
        for i in range(1, nb):
            idx = jnp.where(n == i, jnp.int32(col_blocks[i]), idx)
        return idx

    x_list = [(gain, (1, tn), lambda m, n: (0, n))] if kind == "headnorm" else []
    return _tiled_matmul(
        "proj_" + kind, functools.partial(_proj_epilogue, kind=kind), [u], [(w, wcol)], x_list,
        [((t, nb * tn), out_dtype, (tm, tn), lambda m, n: (m, n))],
        tm=tm, tn=tn, n_m=t // tm, n_n=nb, pairs=[(0, 0)], cast_w=True)[0]


def _sb_kernel(q_ref, k_ref, v_ref, u_ref, o_ref, acc_ref, car_ref, *, tq, n_tiles, n_heads, n_ahead, scale):
    i = pl.program_id(2)
    tk = tq
    tiles, heads = range(n_tiles), range(n_heads)
    hcol = lambda hd: slice(hd * HEAD_DIM, (hd + 1) * HEAD_DIM)
    qrow = lambda ti: slice(ti * tq, (ti + 1) * tq)
    upper2 = u_ref[...]
    causal = lax.broadcasted_iota(jnp.int32, (tq, tk), 1) < lax.broadcasted_iota(jnp.int32, (tq, tk), 0)

    def step(work, carries, accs):
        pairs = [(wi, hd) for wi in range(len(work)) for hd in heads]
        z_, sp_, lb_, part_, later_, before = {}, {}, {}, {}, {}, {}
        for wi, hd in pairs:
            ti, off, _ = work[wi]
            z_[wi, hd] = _dot_nt(q_ref[qrow(ti), hcol(hd)], k_ref[pl.ds(off, tk), hcol(hd)]) * (scale * LOG2E)
        for wi, hd in pairs:
            z = z_[wi, hd]
            sp = jnp.maximum(z, 0.0) + jnp.log2(1.0 + jnp.exp2(-jnp.abs(z)))
            lb_[wi, hd] = z - sp
            if work[wi][2]:
                sp = jnp.where(causal, sp, 0.0)
            sp_hi = sp.astype(BF16)
            sp_[wi, hd] = sp
            part_[wi, hd] = jnp.concatenate([sp_hi, (sp - sp_hi.astype(F32)).astype(BF16)], axis=1)
        for p in pairs:
            later_[p] = _dot(part_[p], upper2)
        carries, accs = dict(carries), dict(accs)
        for wi, hd in pairs:
            ch = (work[wi][0], hd)
            before[wi, hd] = carries[ch]
            carries[ch] = carries[ch] + (later_[wi, hd][:, :1] + sp_[wi, hd][:, :1])
        least = tuple(functools.reduce(jnp.minimum, [carries[ti, hd] for hd in heads]).min()
                      for ti in sorted({w[0] for w in work}))
        for wi, hd in pairs:
            ti, off, on_diag = work[wi]
            w = jnp.exp2(lb_[wi, hd] - later_[wi, hd] - before[wi, hd])
            w = (jnp.where(causal, w, 0.0) if on_diag else w).astype(BF16)
            accs[ti, hd] = accs[ti, hd] + _dot(w, v_ref[pl.ds(off, tk), hcol(hd)])
        return least, carries, accs

    def run(work, carries, accs):
        least, carries, accs = step(work, carries, accs)
        for ti, hd in carries:
            o_ref[qrow(ti), hcol(hd)] = accs[ti, hd].astype(o_ref.dtype)
            acc_ref[ti, hd] = accs[ti, hd]
            car_ref[ti, hd] = carries[ti, hd]
        return least

    chains = [(ti, hd) for ti in tiles for hd in heads]
    zero = ({ch: jnp.zeros((tq, 1), F32) for ch in chains}, {ch: jnp.zeros((tq, HEAD_DIM), F32) for ch in chains})
    first = [i * n_tiles + ti for ti in tiles]
    on_diag = [(ti, pl.multiple_of(first[ti] * tk, tk), True) for ti in tiles]
    ahead = [(ti, pl.multiple_of((first[ti] - 1 - a) * tk, tk), False) for ti in tiles for a in range(n_ahead)]
    usual = first[0] >= n_ahead
    least = lax.cond(usual, lambda: run(on_diag + ahead, *zero), lambda: run(on_diag, *zero))

    def cond(c):
        return jnp.logical_and(c[0] >= 0, c[1] < EXP2_UNDERFLOW)

    for ti in tiles:
        def body(c, ti=ti):
            work = [(ti, pl.multiple_of(c[0] * tk, tk), False)]
            return (c[0] - 1,) + run(work, {(ti, hd): car_ref[ti, hd] for hd in heads},
                                     {(ti, hd): acc_ref[ti, hd] for hd in heads})

        lax.while_loop(cond, body, (first[ti] - 1 - jnp.where(usual, n_ahead, 0), least[ti]))


def _sb_attention(qk, vhi, batch, seq, tq=128, n_tiles=4, hp=4, n_ahead=2):
    t = batch * seq
    rows = n_tiles * tq
    nq = seq // rows
    ng = N_HEADS // hp
    w = hp * HEAD_DIM
    tri = np.tril(np.ones((tq, tq), np.float32), -1)
    upper2 = jnp.asarray(np.concatenate([tri, tri], axis=0), dtype=BF16)
    return pl.pallas_call(
        functools.partial(_sb_kernel, tq=tq, n_tiles=n_tiles, n_heads=hp, n_ahead=n_ahead, scale=HEAD_DIM ** -0.5),
        grid=(batch, ng, nq),
        in_specs=[pl.BlockSpec((rows, w), lambda b, h, i: (b * nq + i, h)),
                  pl.BlockSpec((seq, w), lambda b, h, i: (b, ng + h)),
                  pl.BlockSpec((seq, w), lambda b, h, i: (b, h)),
                  pl.BlockSpec((2 * tq, tq), lambda b, h, i: (0, 0))],
        out_specs=pl.BlockSpec((rows, w), lambda b, h, i: (b * nq + i, h)),
        out_shape=jax.ShapeDtypeStruct((t, MIX_WIDTH), BF16),
        scratch_shapes=[pltpu.VMEM((n_tiles, hp, tq, HEAD_DIM), F32), pltpu.VMEM((n_tiles, hp, tq, 1), F32)],
        compiler_params=_params(("parallel", "parallel", "arbitrary")),
        name="sb_attention",
    )(qk, qk, vhi, upper2)


def _hgrn_cumsum_matrix():
    tri = np.tril(np.ones((HG_CHUNK, HG_CHUNK), np.float32))
    return np.concatenate([tri, tri, tri], axis=1)


def _hgrn_kernel(xq_ref, xf_ref, xo_ref, v_ref, lbl_ref, gout_ref, a_ref, o_ref, st_ref, *, n_chunks, n_heads):
    c_ = HG_CHUNK

    @pl.when(pl.program_id(2) == 0)
    def _():
        st_ref[...] = jnp.zeros_like(st_ref)

    logits = lbl_ref[...]
    e = jnp.exp(logits - jnp.max(logits, axis=0, keepdims=True))
    lb_all = e[0:1, :] / jnp.sum(e, axis=0, keepdims=True)
    a_mat = a_ref[...]
    ti = lax.broadcasted_iota(jnp.int32, (c_, c_), 0)
    si = lax.broadcasted_iota(jnp.int32, (c_, c_), 1)
    tr = lax.broadcasted_iota(jnp.int32, (c_, HEAD_DIM), 0)
    pair_of = {c: ((ti ^ si) < 2 * c) & ((ti & (2 * c - 1)) >= c) & ((si & (2 * c - 1)) < c)
               for c in (32, 16, 8, 4, 2, 1)}
    diag = ti == si
    r4 = tr & 3
    odd = (tr & 1) == 1

    def level_decays(b2, f):
        out = {}
        for c in (32, 16, 8, 4):
            bm = jnp.concatenate(
                [jnp.broadcast_to(b2[s0 + c - 1:s0 + c, :], (2 * c, HEAD_DIM)) for s0 in range(0, c_, 2 * c)], axis=0)
            out[c] = jnp.exp2(-jnp.abs(b2 - bm))
        f_prev = pltpu.roll(f, 1, axis=0)
        f_next = pltpu.roll(f, c_ - 1, axis=0)
        out[2] = jnp.where(r4 == 0, f_next, jnp.where(r4 == 1, 1.0, jnp.where(r4 == 2, f, f_prev * f)))
        out[1] = jnp.where(odd, f, 1.0)
        return out

    pairs = [(ch, hd) for ch in range(n_chunks) for hd in range(n_heads)]
    blk = lambda ch, hd: (slice(ch * c_, (ch + 1) * c_), slice(hd * HEAD_DIM, (hd + 1) * HEAD_DIM))
    f_, kk_, qs_, g3_, b_, s_ = {}, {}, {}, {}, {}, {}
    o_intra, q_in, d_state, e_last = {}, {}, {}, {}
    for p in pairs:
        rows, cols = blk(*p)
        lb = lb_all[:, cols]
        xq = xq_ref[rows, cols]
        f = lb + (1.0 - lb) * _sigmoid(xf_ref[rows, cols])
        g = jnp.log(f)
        g_hi = g.astype(BF16)
        r = g - g_hi.astype(F32)
        g_mid = r.astype(BF16)
        g_lo = (r - g_mid.astype(F32)).astype(BF16)
        f_[p], kk_[p], qs_[p] = f, 1.0 - f, xq * _sigmoid(xq)
        g3_[p] = jnp.concatenate([g_hi, g_mid, g_lo], axis=0)
    for p in pairs:
        b_[p] = _dot(a_mat, g3_[p]) * LOG2E
    for p in pairs:
        qs, kk = qs_[p].astype(BF16), kk_[p].astype(BF16)
        s = jnp.where(diag, _dot_nt(qs, kk), 0.0)
        for c, decay in level_decays(b_[p], f_[p]).items():
            decay = decay.astype(BF16)
            s = jnp.where(pair_of[c], _dot_nt(qs * decay, kk * decay), s)
        s_[p] = s.astype(BF16)
    for p in pairs:
        rows, cols = blk(*p)
        v = v_ref[rows, cols]
        b2 = b_[p]
        b2_last = b2[c_ - 1:c_, :]
        o_intra[p] = _dot(s_[p], v)
        q_in[p] = qs_[p].astype(BF16) * jnp.exp2(b2).astype(BF16)
        d_state[p] = _dot_tn(v, kk_[p].astype(BF16) * jnp.exp2(b2_last - b2).astype(BF16))
        e_last[p] = jnp.exp2(b2_last)

    for hd in range(n_heads):
        cols = slice(hd * HEAD_DIM, (hd + 1) * HEAD_DIM)
        st = st_ref[hd]
        for ch in range(n_chunks):
            rows = slice(ch * c_, (ch + 1) * c_)
            o = o_intra[ch, hd] + _dot_nt(q_in[ch, hd], st.astype(BF16))
            st = st * e_last[ch, hd] + d_state[ch, hd]
            xo = xo_ref[rows, cols]
            o_ref[rows, cols] = (_rms_scale(o) * gout_ref[...] * (xo * _sigmoid(xo))).astype(o_ref.dtype)
        st_ref[hd] = st


def _hgrn2(hgf, vhi, lb_logits, g_out, batch, seq, tc=1024, hp=4):
    t = batch * seq
    tc = min(tc, seq)
    ns = seq // tc
    ng = N_HEADS // hp
    w = hp * HEAD_DIM
    a_mat = jnp.asarray(_hgrn_cumsum_matrix(), dtype=BF16)
    row = lambda off: (lambda b, h, i: (b * ns + i, off + h))
    return pl.pallas_call(
        functools.partial(_hgrn_kernel, n_chunks=tc // HG_CHUNK, n_heads=hp),
        grid=(batch, ng, ns),
        in_specs=[pl.BlockSpec((tc, w), row(0)),
                  pl.BlockSpec((tc, w), row(ng)),
                  pl.BlockSpec((tc, w), row(2 * ng)),
                  pl.BlockSpec((tc, w), row(ng)),
                  pl.BlockSpec((lb_logits.shape[0], w), lambda b, h, i: (0, h)),
                  pl.BlockSpec((1, HEAD_DIM), lambda b, h, i: (0, 0)),
                  pl.BlockSpec(a_mat.shape, lambda b, h, i: (0, 0))],
        out_specs=pl.BlockSpec((tc, w), row(0)),
        out_shape=jax.ShapeDtypeStruct((t, MIX_WIDTH), BF16),
        scratch_shapes=[pltpu.VMEM((hp, HEAD_DIM, HEAD_DIM), F32)],
        compiler_params=_params(("parallel", "parallel", "arbitrary")),
        name="hgrn2",
    )(hgf, hgf, hgf, vhi, lb_logits, g_out.reshape(1, HEAD_DIM), a_mat)


def _merge_epilogue(dots, x_refs, o_refs, scratch):
    ga, gb = x_refs
    ma = ga[...].astype(F32) * dots[0]()
    o_refs[0][...] = (ma + gb[...].astype(F32) * dots[1]()).astype(BF16)


def _merge(ya, yb, pa, pb, gates, tm=1024, tn=1024):
    t, _ = ya.shape
    d = pa.shape[1]
    tm = min(tm, t)
    nn = d // tn
    return _tiled_matmul(
        "merge", _merge_epilogue, [ya, yb], [(pa, lambda n: n), (pb, lambda n: n)],
        [(gates, (tm, tn), lambda m, n: (m, n)), (gates, (tm, tn), lambda m, n: (m, nn + n))],
        [((t, d), BF16, (tm, tn), lambda m, n: (m, n))],
        tm=tm, tn=tn, n_m=t // tm, n_n=nn, pairs=[(0, 0), (1, 1)], cast_w=True)[0]


def _out_proj_epilogue(dots, x_refs, o_refs, scratch):
    x_ref, g_ref = x_refs
    h = x_ref[...] + dots[0]()
    o_refs[0][...] = h
    o_refs[1][...] = (_rms_scale(h) * g_ref[...]).astype(BF16)


def _out_proj(m, w, x, g, tm=512):
    t, d = x.shape
    tm = min(tm, t)
    return _tiled_matmul(
        "out_proj", _out_proj_epilogue, [m], [(w, lambda n: n)],
        [(x, (tm, d), lambda m_, n: (m_, 0)), (g.reshape(1, d), (1, d), lambda m_, n: (0, 0))],
        [((t, d), F32, (tm, d), lambda m_, n: (m_, 0)), ((t, d), BF16, (tm, d), lambda m_, n: (m_, 0))],
        tm=tm, tn=d, n_m=t // tm, n_n=1, pairs=[(0, 0)])


def _ffn_up_epilogue(dots, x_refs, o_refs, scratch, *, tm, tiles_per_seq):
    cwg_ref, cwv_ref, cbg_ref, cbv_ref = x_refs
    hg_ref, hv_ref = scratch

    @pl.when(pl.program_id(1) % tiles_per_seq == 0)
    def _():
        hg_ref[...] = jnp.zeros_like(hg_ref)
        hv_ref[...] = jnp.zeros_like(hv_ref)

    def conv(up, halo, cw_ref, cb_ref):
        edge = jnp.concatenate([halo, up[0:CONV_HALO, :]], axis=0)
        out = cb_ref[...] + cw_ref[2:3, :] * up
        for j in (1, 2):
            shifted = jnp.concatenate([edge[CONV_HALO - j:2 * CONV_HALO - j, :],
                                       pltpu.roll(up, j, axis=0)[CONV_HALO:, :]], axis=0)
            out = out + cw_ref[2 - j:3 - j, :] * shifted
        return out

    up_g = dots[0]()
    gate = conv(up_g, hg_ref[...], cwg_ref, cbg_ref)
    up_v = dots[1]()
    val = conv(up_v, hv_ref[...], cwv_ref, cbv_ref)
    o_refs[0][...] = (gate * _sigmoid(gate) * val).astype(BF16)
    hg_ref[...] = up_g[tm - CONV_HALO:, :]
    hv_ref[...] = up_v[tm - CONV_HALO:, :]


def _ffn_up(hn, w_up, conv_w, conv_b, seq, tm=1024, tn=512):
    t, _ = hn.shape
    dff = w_up.shape[1] // 2
    tm = min(tm, seq)
    nn = dff // tn
    conv_b = conv_b.reshape(1, -1)
    return _tiled_matmul(
        "ffn_up", functools.partial(_ffn_up_epilogue, tm=tm, tiles_per_seq=seq // tm),
        [hn], [(w_up, lambda n: n), (w_up, lambda n: nn + n)],
        [(conv_w, (CONV_WIDTH, tn), lambda m, n: (0, n)), (conv_w, (CONV_WIDTH, tn), lambda m, n: (0, nn + n)),
         (conv_b, (1, tn), lambda m, n: (0, n)), (conv_b, (1, tn), lambda m, n: (0, nn + n))],
        [((t, dff), BF16, (tm, tn), lambda m, n: (m, n))],
        tm=tm, tn=tn, n_m=t // tm, n_n=nn, pairs=[(0, 0), (0, 1)], cast_w=True,
        scratch=[pltpu.VMEM((CONV_HALO, tn), F32)] * 2)[0]


def _ffn_down_epilogue(dots, x_refs, o_refs, scratch):
    o_refs[0][...] = x_refs[0][...] + dots[0]()


def _ffn_down(act, w, h, tm=512, tn=512):
    t, _ = act.shape
    d = w.shape[1]
    tm = min(tm, t)
    return _tiled_matmul(
        "ffn_down", _ffn_down_epilogue, [act], [(w, lambda n: n)],
        [(h, (tm, tn), lambda m, n: (m, n))],
        [((t, d), F32, (tm, tn), lambda m, n: (m, n))],
        tm=tm, tn=tn, n_m=t // tm, n_n=d // tn, pairs=[(0, 0)], cast_w=True)[0]


def kernel(x, g_mix, w_in, g_q, g_k, lb_logits, g_hg_out, p_a, p_b, w_o, g_ffn, w_up, conv_w, conv_b, w_down):
    batch, seq, d = x.shape
    depth = g_mix.shape[0]
    assert depth == 1 and lb_logits.shape == (depth + 1, MIX_WIDTH)
    h = x.reshape(batch * seq, d)
    for layer in range(depth):
        u = _rmsnorm(h, g_mix[layer], tm=min(512, seq))
        qk_gain = jnp.concatenate([jnp.tile(g_q[layer], N_HEADS), jnp.tile(g_k[layer], N_HEADS)]).reshape(1, -1)
        qk = _proj(u, w_in[layer], (0, 1), MIX_WIDTH, BF16, "headnorm", gain=qk_gain)
        vhi = _proj(u, w_in[layer], (2, 5), MIX_WIDTH, BF16, "cast")
        hgf = _proj(u, w_in[layer], (3, 4, 6), MIX_WIDTH, F32, "cast")
        gates = _proj(u, w_in[layer], (7, 8, 9, 10), MIX_WIDTH, BF16, "sigmoid")

        y_a = _sb_attention(qk, vhi, batch, seq)
        y_b = _hgrn2(hgf, vhi, lb_logits, g_hg_out[layer], batch, seq)

        m = _merge(y_a, y_b, p_a[layer], p_b[layer], gates)
        h, hn = _out_proj(m, w_o[layer].astype(BF16), h, g_ffn[layer])
        act = _ffn_up(hn, w_up[layer], conv_w[layer], conv_b[layer], seq)
        h = _ffn_down(act, w_down[layer], h)
    return h.reshape(batch, seq, d)
```
